```python
import math
import jax, jax.numpy as jnp
from jax import lax
import numpy as np

D_MODEL = 4096
BATCH = 4
SEQ = 2048
DEPTH = 2
DEC_BATCH = 8
DEC_SEQ = 1
PAST_LEN = 16384
PAGE_SIZE = 128

SSD_HEADS = 32
SSD_HEAD_DIM = 64
SSD_INNER = SSD_HEADS * SSD_HEAD_DIM
SSD_GROUPS = 4
SSD_STATE = 128
SSD_CONV = 4
SSD_CHUNK = 128
SSD_CONV_DIM = SSD_INNER + 2 * SSD_GROUPS * SSD_STATE
POOL_WINDOWS = (2, 4, 8, 16)
POOL_GROUPS = 4
POOL_WIDTH = 2048
POOL_GROUP_DIM = POOL_WIDTH // POOL_GROUPS
POOL_BUF = max(POOL_WINDOWS) - 1
ATT_HEADS = 16
ATT_HEAD_DIM = 128
ATT_WIDTH = ATT_HEADS * ATT_HEAD_DIM
MOBA_BLOCK = 256
MOBA_TOPK = 3
MOBA_Q_BLOCK = 32
REL_BUCKETS = 32
REL_MAX_EXACT = 16
REL_MAX_DIST = 128
PEER_HEADS = 8
PEER_KEYS = 128
PEER_EXPERTS = PEER_KEYS * PEER_KEYS
PEER_QDIM = 256
PEER_HALF = PEER_QDIM // 2
PEER_TOPK = 16
PEER_TOKEN_BLOCK = 128
DEEPNORM_ALPHA = (2 * DEPTH) ** 0.25
DEEPNORM_BETA = (8 * DEPTH) ** -0.25
LN_EPS = 1e-5
OFF_Z = 0
OFF_XBC = OFF_Z + SSD_INNER
OFF_DT = OFF_XBC + SSD_CONV_DIM
OFF_POOL = OFF_DT + SSD_HEADS
OFF_Q = OFF_POOL + POOL_WIDTH
OFF_K = OFF_Q + ATT_WIDTH
OFF_V = OFF_K + ATT_WIDTH
OFF_GATE = OFF_V + ATT_WIDTH
PROJ_DIM = OFF_GATE + 3 * D_MODEL

kernel_name = 'hybrid_ssd_pool_moba_peer_decoder_step'

F32 = jnp.float32


def layer_norm(x, g, b):
    xf = x.astype(F32)
    mu = jnp.mean(xf, axis=-1, keepdims=True)
    var = jnp.mean(jnp.square(xf - mu), axis=-1, keepdims=True)
    return ((xf - mu) * lax.rsqrt(var + LN_EPS) * g.astype(F32) + b.astype(F32)).astype(x.dtype)


def causal_conv(xbc, buf, w, b):
    L = xbc.shape[1]
    xp = jnp.concatenate([buf.astype(xbc.dtype), xbc], axis=1)
    out = b + xp[:, 0:L] * w[:, 0]
    for tap in range(1, SSD_CONV):
        out = out + xp[:, tap:tap + L] * w[:, tap]
    return out, xp[:, L:]


def ssd_chunked(xdt, a, bm, cm, h0):
    b, L, H, P = xdt.shape
    G, N = bm.shape[2], bm.shape[3]
    R = H // G
    l = SSD_CHUNK if L % SSD_CHUNK == 0 else L
    c = L // l
    X = xdt.reshape(b, c, l, G, R, P)
    A = a.reshape(b, c, l, G, R)
    Bc = bm.reshape(b, c, l, G, N)
    Cc = cm.reshape(b, c, l, G, N)
    a_cs = jnp.cumsum(A, axis=2)
    seg = a_cs[:, :, :, None] - a_cs[:, :, None, :]
    causal = jnp.tril(jnp.ones((l, l), bool))[None, None, :, :, None, None]
    decay_in = jnp.exp(jnp.where(causal, seg, -jnp.inf))
    cb = jnp.einsum('bcign,bcjgn->bcijg', Cc, Bc)
    y_diag = jnp.einsum('bcijgr,bcjgrp->bcigrp', cb[..., None] * decay_in, X)
    decay_to_end = jnp.exp(a_cs[:, :, -1:] - a_cs)
    chunk_states = jnp.einsum('bclgn,bclgrp->bcgrpn', Bc, X * decay_to_end[..., None])
    chunk_decay = jnp.exp(a_cs[:, :, -1])

    def carry(h, inp):
        st, dec = inp
        return h * dec[..., None, None] + st, h

    h_last, h_in = lax.scan(carry, h0.reshape(b, G, R, P, N),
                            (jnp.moveaxis(chunk_states, 1, 0), jnp.moveaxis(chunk_decay, 1, 0)))
    h_in = jnp.moveaxis(h_in, 0, 1)
    y_off = jnp.einsum('bclgn,bcgrpn->bclgrp', Cc, h_in) * jnp.exp(a_cs)[..., None]
    y = (y_diag + y_off).reshape(b, L, H, P)
    return y, h_last.reshape(b, H, P, N)


def ssd_mixer(z, xbc_raw, dt_raw, conv_buf, h0, conv_w, conv_b, dt_bias, a_log, d_skip, norm_w):
    bsz, L, _ = xbc_raw.shape
    xbc, conv_new = causal_conv(xbc_raw, conv_buf, conv_w, conv_b)
    xbc = jax.nn.silu(xbc.astype(F32))
    gn = SSD_GROUPS * SSD_STATE
    xs = xbc[..., :SSD_INNER].reshape(bsz, L, SSD_HEADS, SSD_HEAD_DIM)
    bm = xbc[..., SSD_INNER:SSD_INNER + gn].reshape(bsz, L, SSD_GROUPS, SSD_STATE)
    cm = xbc[..., SSD_INNER + gn:].reshape(bsz, L, SSD_GROUPS, SSD_STATE)
    dt = jax.nn.softplus(dt_raw.astype(F32) + dt_bias.astype(F32))
    a = -jnp.exp(a_log.astype(F32))
    y, h_new = ssd_chunked(xs * dt[..., None], dt * a, bm, cm, h0.astype(F32))
    y = y + d_skip.astype(F32)[:, None] * xs
    y = y.reshape(bsz, L, SSD_INNER) * jax.nn.silu(z.astype(F32))
    yg = y.reshape(bsz, L, SSD_GROUPS, SSD_INNER // SSD_GROUPS)
    yg = yg * lax.rsqrt(jnp.mean(jnp.square(yg), axis=-1, keepdims=True) + LN_EPS)
    y = yg.reshape(bsz, L, SSD_INNER) * norm_w.astype(F32)
    return y.astype(z.dtype), conv_new, h_new.astype(z.dtype)


def pool_mixer(p, buf, start, w_grp, scale):
    bsz, L, _ = p.shape
    xcat = jnp.concatenate([buf.astype(p.dtype), p], axis=1)
    csum = jnp.cumsum(xcat.astype(F32), axis=1)
    csum = jnp.concatenate([jnp.zeros((bsz, 1, POOL_WIDTH), F32), csum], axis=1)
    pos = start + jnp.arange(L)
    outs = []
    for g, w in enumerate(POOL_WINDOWS):
        sl = slice(g * POOL_GROUP_DIM, (g + 1) * POOL_GROUP_DIM)
        win = csum[:, POOL_BUF + 1:POOL_BUF + 1 + L, sl] - csum[:, POOL_BUF + 1 - w:POOL_BUF + 1 - w + L, sl]
        cnt = jnp.minimum(pos + 1, w).astype(F32)
        outs.append(win / cnt[None, :, None])
    pooled = (jnp.concatenate(outs, axis=-1) - p.astype(F32)).astype(p.dtype)
    mixed = jnp.einsum('blgc,gcd->blgd', pooled.reshape(bsz, L, POOL_GROUPS, POOL_GROUP_DIM), w_grp)
    return mixed.reshape(bsz, L, POOL_WIDTH) * scale, xcat[:, L:]


def rel_bucket(dist):
    n = jnp.maximum(dist, 0)
    large = REL_MAX_EXACT + (jnp.log(jnp.maximum(n, 1).astype(F32) / REL_MAX_EXACT)
                             / math.log(REL_MAX_DIST / REL_MAX_EXACT)
                             * (REL_BUCKETS - REL_MAX_EXACT)).astype(jnp.int32)
    large = jnp.minimum(large, REL_BUCKETS - 1)
    return jnp.where(n < REL_MAX_EXACT, n, large)


def moba_attention(q, k, v, q_start, rel_bias):
    bsz, lq, H, hd = q.shape
    qb = lq if lq < MOBA_Q_BLOCK else MOBA_Q_BLOCK
    n_qb = -(-lq // qb)
    lq_pad = n_qb * qb
    nb = -(-(q_start + lq_pad) // MOBA_BLOCK)
    lk_pad = nb * MOBA_BLOCK
    q = jnp.pad(q, ((0, 0), (0, lq_pad - lq), (0, 0), (0, 0)))
    k = jnp.pad(k, ((0, 0), (0, lk_pad - k.shape[1]), (0, 0), (0, 0)))
    v = jnp.pad(v, ((0, 0), (0, lk_pad - v.shape[1]), (0, 0), (0, 0)))
    kb = k.reshape(bsz, nb, MOBA_BLOCK, H, hd)
    vb = v.reshape(bsz, nb, MOBA_BLOCK, H, hd)
    k_mean = jnp.mean(kb, axis=2, dtype=F32)
    qpos = q_start + jnp.arange(lq_pad)
    n_past = qpos // MOBA_BLOCK
    gate = jnp.einsum('bqhd,bnhd->bqhn', q.astype(F32), k_mean)
    gate = jnp.where((jnp.arange(nb)[None, :] < n_past[:, None])[None, :, None, :], gate, -jnp.inf)
    n_sel = min(MOBA_TOPK, nb)
    _, sel = lax.top_k(gate, n_sel)
    scale = hd ** -0.5
    h_idx = jnp.arange(H)[None, :, None]
    h_idx4 = jnp.arange(H)[None, :, None, None]
    blk_ar = jnp.arange(MOBA_BLOCK)

    def one_block(args):
        bi, qi, q_blk, sel_blk = args
        qp = q_start + qi * qb + jnp.arange(qb)
        npast = qp // MOBA_BLOCK
        k_sel = kb[bi, sel_blk, :, h_idx]
        v_sel = vb[bi, sel_blk, :, h_idx]
        kpos_sel = sel_blk[..., None] * MOBA_BLOCK + blk_ar
        b_sel = rel_bias[rel_bucket(qp[:, None, None, None] - kpos_sel), h_idx4]
        l_sel = jnp.einsum('qhd,qhnld->qhnl', q_blk, k_sel, preferred_element_type=F32) * scale + b_sel
        valid = (jnp.arange(n_sel)[None, :] < npast[:, None])[:, None, :, None]
        l_sel = jnp.where(valid, l_sel, -jnp.inf)
        own = (q_start + qi * qb) // MOBA_BLOCK
        k_own = kb[bi, own]
        v_own = vb[bi, own]
        dist = qp[:, None] - (own * MOBA_BLOCK + blk_ar)[None, :]
        b_own = jnp.transpose(rel_bias[rel_bucket(dist)], (0, 2, 1))
        l_own = jnp.einsum('qhd,lhd->qhl', q_blk, k_own, preferred_element_type=F32) * scale + b_own
        l_own = jnp.where((dist >= 0)[:, None, :], l_own, -jnp.inf)
        logits = jnp.concatenate([l_sel.reshape(qb, H, n_sel * MOBA_BLOCK), l_own], axis=-1)
        prob = jax.nn.softmax(logits, axis=-1)
        p_sel = prob[..., :n_sel * MOBA_BLOCK].reshape(qb, H, n_sel, MOBA_BLOCK).astype(v.dtype)
        p_own = prob[..., n_sel * MOBA_BLOCK:].astype(v.dtype)
        return jnp.einsum('qhnl,qhnld->qhd', p_sel, v_sel) + jnp.einsum('qhl,lhd->qhd', p_own, v_own)

    flat = jnp.arange(bsz * n_qb)
    out = lax.map(one_block, (flat // n_qb, flat % n_qb,
                              q.reshape(bsz * n_qb, qb, H, hd),
                              sel.reshape(bsz * n_qb, qb, H, n_sel)))
    return out.reshape(bsz, lq_pad, H, hd)[:, :lq]


def peer(x, wq, subkeys, u_tab, v_tab):
    T = x.shape[0]
    q = (x @ wq).reshape(T, PEER_HEADS, 2, PEER_HALF)
    s = jnp.einsum('thcd,hckd->thck', q, subkeys, preferred_element_type=F32)
    s1, i1 = lax.top_k(s[:, :, 0], PEER_TOPK)
    s2, i2 = lax.top_k(s[:, :, 1], PEER_TOPK)
    cand_s = (s1[..., :, None] + s2[..., None, :]).reshape(T, PEER_HEADS, PEER_TOPK * PEER_TOPK)
    cand_i = (i1[..., :, None] * PEER_KEYS + i2[..., None, :]).reshape(T, PEER_HEADS, PEER_TOPK * PEER_TOPK)
    top_s, pos = lax.top_k(cand_s, PEER_TOPK)
    e_idx = jnp.take_along_axis(cand_i, pos, axis=-1).reshape(T, PEER_HEADS * PEER_TOPK)
    gate = jax.nn.softmax(top_s, axis=-1).reshape(T, PEER_HEADS * PEER_TOPK)
    tb = min(PEER_TOKEN_BLOCK, T)
    n_blk = -(-T // tb)
    pad = n_blk * tb - T
    xp = jnp.pad(x, ((0, pad), (0, 0))).reshape(n_blk, tb, D_MODEL)
    ep = jnp.pad(e_idx, ((0, pad), (0, 0))).reshape(n_blk, tb, PEER_HEADS * PEER_TOPK)
    gp = jnp.pad(gate, ((0, pad), (0, 0))).reshape(n_blk, tb, PEER_HEADS * PEER_TOPK)

    def blk(args):
        xb, eb, gb = args
        ue = u_tab[eb]
        hidden = jax.nn.gelu(jnp.einsum('tkd,td->tk', ue, xb, preferred_element_type=F32))
        ve = v_tab[eb]
        return jnp.einsum('tk,tkd->td', (gb * hidden).astype(x.dtype), ve)

    out = lax.map(blk, (xp, ep, gp)).reshape(n_blk * tb, D_MODEL)
    return out[:T]


def decoder_layer(x, start, conv_buf, ssm_h0, pool_buf, k_past, v_past, lp):
    bsz, L, _ = x.shape
    proj = x @ lp['w_in']
    y_ssd, conv_new, h_new = ssd_mixer(proj[..., OFF_Z:OFF_XBC], proj[..., OFF_XBC:OFF_DT],
                                       proj[..., OFF_DT:OFF_POOL], conv_buf, ssm_h0,
                                       lp['conv_w'], lp['conv_b'], lp['dt_bias'], lp['a_log'],
                                       lp['d_skip'], lp['ssd_norm_w'])
    y_pool, pool_new = pool_mixer(proj[..., OFF_POOL:OFF_Q], pool_buf, start, lp['pool_w'], lp['pool_scale'])
    q = proj[..., OFF_Q:OFF_K].reshape(bsz, L, ATT_HEADS, ATT_HEAD_DIM)
    k = proj[..., OFF_K:OFF_V].reshape(bsz, L, ATT_HEADS, ATT_HEAD_DIM)
    v = proj[..., OFF_V:OFF_GATE].reshape(bsz, L, ATT_HEADS, ATT_HEAD_DIM)
    k_all = k if k_past is None else jnp.concatenate([k_past.astype(k.dtype), k], axis=1)
    v_all = v if v_past is None else jnp.concatenate([v_past.astype(v.dtype), v], axis=1)
    y_att = moba_attention(q, k_all, v_all, start, lp['rel_bias']).reshape(bsz, L, ATT_WIDTH)
    gates = jax.nn.sigmoid(proj[..., OFF_GATE:].astype(F32)).astype(x.dtype).reshape(bsz, L, 3, D_MODEL)
    merged = (gates[:, :, 0] * (y_ssd @ lp['w_br_ssd'])
              + gates[:, :, 1] * (y_pool @ lp['w_br_pool'])
              + gates[:, :, 2] * (y_att @ lp['w_br_att']))
    x = layer_norm(DEEPNORM_ALPHA * x + merged @ lp['w_out'], lp['ln1_g'], lp['ln1_b'])
    ffn = peer(x.reshape(bsz * L, D_MODEL), lp['peer_wq'], lp['peer_subkeys'],
               lp['peer_u'], lp['peer_v']).reshape(bsz, L, D_MODEL)
    x = layer_norm(DEEPNORM_ALPHA * x + ffn, lp['ln2_g'], lp['ln2_b'])
    return x, conv_new, h_new, pool_new, k, v


def setup_inputs(seed: int = 0) -> dict:
    key = jax.random.key(seed)
    ks = jax.random.split(key, 40)
    nrm = lambda k, shape, s: jax.random.normal(k, shape, F32) * s
    n_pages = PAST_LEN // PAGE_SIZE
    n_pool = (DEC_BATCH * n_pages * 5 + 3) // 4
    page_table = jax.random.permutation(ks[0], n_pool)[:DEC_BATCH * n_pages].reshape(DEC_BATCH, n_pages).astype(jnp.int32)
    dt0 = jnp.exp(jax.random.uniform(ks[1], (DEPTH, SSD_HEADS), F32, math.log(1e-3), math.log(1e-1)))
    return {
        'x_prompt': nrm(ks[2], (BATCH, SEQ, D_MODEL), 1.0),
        'x_sample': nrm(ks[3], (DEC_BATCH, DEC_SEQ, D_MODEL), 1.0),
        'cache_k': nrm(ks[4], (n_pool, DEPTH, PAGE_SIZE, ATT_HEADS, ATT_HEAD_DIM), 1.0),
        'cache_v': nrm(ks[5], (n_pool, DEPTH, PAGE_SIZE, ATT_HEADS, ATT_HEAD_DIM), 1.0),
        'state_ssm': nrm(ks[6], (DEPTH, DEC_BATCH, SSD_HEADS, SSD_HEAD_DIM, SSD_STATE), 0.1),
        'state_conv': nrm(ks[7], (DEPTH, DEC_BATCH, SSD_CONV - 1, SSD_CONV_DIM), 1.0),
        'state_pool': nrm(ks[8], (DEPTH, DEC_BATCH, POOL_BUF, POOL_WIDTH), 1.0),
        'page_table': page_table,
        'w_in': nrm(ks[9], (DEPTH, D_MODEL, PROJ_DIM), D_MODEL ** -0.5),
        'conv_w': nrm(ks[10], (DEPTH, SSD_CONV_DIM, SSD_CONV), SSD_CONV ** -0.5),
        'conv_b': nrm(ks[11], (DEPTH, SSD_CONV_DIM), 0.01),
        'dt_bias': dt0 + jnp.log(-jnp.expm1(-dt0)),
        'a_log': jnp.log(jax.random.uniform(ks[12], (DEPTH, SSD_HEADS), F32, 1.0, 16.0)),
        'd_skip': 1.0 + nrm(ks[13], (DEPTH, SSD_HEADS), 0.1),
        'ssd_norm_w': 1.0 + nrm(ks[14], (DEPTH, SSD_INNER), 0.02),
        'pool_w': nrm(ks[15], (DEPTH, POOL_GROUPS, POOL_GROUP_DIM, POOL_GROUP_DIM), POOL_GROUP_DIM ** -0.5),
        'pool_scale': 1.0 + nrm(ks[16], (DEPTH, POOL_WIDTH), 0.02),
        'rel_bias': nrm(ks[17], (REL_BUCKETS, ATT_HEADS), 0.5),
        'w_br_ssd': nrm(ks[18], (DEPTH, SSD_INNER, D_MODEL), SSD_INNER ** -0.5 * DEEPNORM_BETA),
        'w_br_pool': nrm(ks[19], (DEPTH, POOL_WIDTH, D_MODEL), POOL_WIDTH ** -0.5 * DEEPNORM_BETA),
        'w_br_att': nrm(ks[20], (DEPTH, ATT_WIDTH, D_MODEL), ATT_WIDTH ** -0.5 * DEEPNORM_BETA),
        'w_out': nrm(ks[21], (DEPTH, D_MODEL, D_MODEL), D_MODEL ** -0.5 * DEEPNORM_BETA),
        'ln1_g': 1.0 + nrm(ks[22], (DEPTH, D_MODEL), 0.02),
        'ln1_b': nrm(ks[23], (DEPTH, D_MODEL), 0.01),
        'peer_wq': nrm(ks[24], (DEPTH, D_MODEL, PEER_HEADS * PEER_QDIM), D_MODEL ** -0.5),
        'peer_subkeys': nrm(ks[25], (DEPTH, PEER_HEADS, 2, PEER_KEYS, PEER_HALF), PEER_HALF ** -0.5),
        'peer_u': nrm(ks[26], (DEPTH, PEER_EXPERTS, D_MODEL), D_MODEL ** -0.5),
        'peer_v': nrm(ks[27], (DEPTH, PEER_EXPERTS, D_MODEL), DEEPNORM_BETA * PEER_HEADS ** -0.5),
        'ln2_g': 1.0 + nrm(ks[28], (DEPTH, D_MODEL), 0.02),
        'ln2_b': nrm(ks[29], (DEPTH, D_MODEL), 0.01),
    }


def reference(x_prompt, x_sample, cache_k, cache_v, state_ssm, state_conv, state_pool, page_table,
              w_in, conv_w, conv_b, dt_bias, a_log, d_skip, ssd_norm_w, pool_w, pool_scale, rel_bias,
              w_br_ssd, w_br_pool, w_br_att, w_out, ln1_g, ln1_b, peer_wq, peer_subkeys, peer_u, peer_v,
              ln2_g, ln2_b):
    bp = x_prompt.shape[0]
    nd = x_sample.shape[0]
    past_len = page_table.shape[1] * cache_k.shape[2]
    dt_ = x_prompt.dtype
    zero_conv = jnp.zeros((bp, SSD_CONV - 1, SSD_CONV_DIM), dt_)
    zero_ssm = jnp.zeros((bp, SSD_HEADS, SSD_HEAD_DIM, SSD_STATE), dt_)
    zero_pool = jnp.zeros((bp, POOL_BUF, POOL_WIDTH), dt_)
    yp, ys = x_prompt, x_sample
    kp, vp, ksm, vsm, hp, hs, cp, cs, pp, ps = [], [], [], [], [], [], [], [], [], []
    for i in range(DEPTH):
        lp = {'w_in': w_in[i], 'conv_w': conv_w[i], 'conv_b': conv_b[i], 'dt_bias': dt_bias[i],
              'a_log': a_log[i], 'd_skip': d_skip[i], 'ssd_norm_w': ssd_norm_w[i], 'pool_w': pool_w[i],
              'pool_scale': pool_scale[i], 'rel_bias': rel_bias, 'w_br_ssd': w_br_ssd[i],
              'w_br_pool': w_br_pool[i], 'w_br_att': w_br_att[i], 'w_out': w_out[i],
              'ln1_g': ln1_g[i], 'ln1_b': ln1_b[i], 'peer_wq': peer_wq[i], 'peer_subkeys': peer_subkeys[i],
              'peer_u': peer_u[i], 'peer_v': peer_v[i], 'ln2_g': ln2_g[i], 'ln2_b': ln2_b[i]}
        yp, c_new, h_new, p_new, k_new, v_new = decoder_layer(yp, 0, zero_conv, zero_ssm, zero_pool, None, None, lp)
        kp.append(k_new); vp.append(v_new); hp.append(h_new); cp.append(c_new); pp.append(p_new)
        k_past = cache_k[page_table, i].reshape(nd, past_len, ATT_HEADS, ATT_HEAD_DIM)
        v_past = cache_v[page_table, i].reshape(nd, past_len, ATT_HEADS, ATT_HEAD_DIM)
        ys, c_new, h_new, p_new, k_new, v_new = decoder_layer(ys, past_len, state_conv[i], state_ssm[i],
                                                              state_pool[i], k_past, v_past, lp)
        ksm.append(k_new); vsm.append(v_new); hs.append(h_new); cs.append(c_new); ps.append(p_new)
    k_prompt = jnp.stack(kp, axis=1)
    v_prompt = jnp.stack(vp, axis=1)
    k_sample = jnp.stack(ksm, axis=1)
    v_sample = jnp.stack(vsm, axis=1)
    ssm_prompt = jnp.stack(hp, axis=0)
    ssm_sample = jnp.stack(hs, axis=0)
    conv_prompt = jnp.stack(cp, axis=0)
    conv_sample = jnp.stack(cs, axis=0)
    pool_prompt = jnp.stack(pp, axis=0)
    pool_sample = jnp.stack(ps, axis=0)
    return (yp, ys, k_prompt, v_prompt, k_sample, v_sample, ssm_prompt, ssm_sample,
            conv_prompt, conv_sample, pool_prompt, pool_sample)
```

```python
import functools
import math

import jax
import jax.numpy as jnp
from jax import lax
from jax.experimental import pallas as pl
from jax.experimental.pallas import tpu as pltpu

F32 = jnp.float32
BF16 = jnp.bfloat16

D_MODEL = 4096
DEPTH = 2
SSD_HEADS = 32
SSD_HEAD_DIM = 64
SSD_INNER = SSD_HEADS * SSD_HEAD_DIM
SSD_GROUPS = 4
SSD_STATE = 128
SSD_CONV = 4
SSD_CONV_DIM = SSD_INNER + 2 * SSD_GROUPS * SSD_STATE
POOL_WINDOWS = (2, 4, 8, 16)
POOL_GROUPS = 4
POOL_WIDTH = 2048
POOL_GROUP_DIM = POOL_WIDTH // POOL_GROUPS
POOL_BUF = max(POOL_WINDOWS) - 1
ATT_HEADS = 16
ATT_HEAD_DIM = 128
ATT_WIDTH = ATT_HEADS * ATT_HEAD_DIM
MOBA_BLOCK = 256
MOBA_TOPK = 3
REL_BUCKETS = 32
REL_MAX_EXACT = 16
REL_MAX_DIST = 128
PEER_HEADS = 8
PEER_KEYS = 128
PEER_EXPERTS = PEER_KEYS * PEER_KEYS
PEER_QDIM = 256
PEER_HALF = PEER_QDIM // 2
PEER_TOPK = 16
DEEPNORM_ALPHA = (2 * DEPTH) ** 0.25
LN_EPS = 1e-5

OFF_Z = 0
OFF_XBC = OFF_Z + SSD_INNER
OFF_DT = OFF_XBC + SSD_CONV_DIM
OFF_POOL = OFF_DT + SSD_HEADS
OFF_Q = OFF_POOL + POOL_WIDTH
OFF_K = OFF_Q + ATT_WIDTH
OFF_V = OFF_K + ATT_WIDTH
OFF_GATE = OFF_V + ATT_WIDTH
PROJ_DIM = OFF_GATE + 3 * D_MODEL

LANE = 128
SUBLANE = 8
VMEM_LIMIT = 56 * 1024 * 1024

R_Z = 0
R_XS = 2048
R_POOL = 4096
R_Q = 6144
R_K = 8192
R_V = 10240
R_GATE = 12288
R_BC = R_GATE + 3 * D_MODEL
R_DT = R_BC + 2 * SSD_GROUPS * SSD_STATE
R_END = R_DT + LANE
PROJ_TN = 512
R_DIM = -(-R_END // PROJ_TN) * PROJ_TN

CHUNK = 128
CONV_TAIL = SUBLANE
POOL_TAIL = 16


def _params(*sem):
    return pltpu.CompilerParams(dimension_semantics=sem, vmem_limit_bytes=VMEM_LIMIT)


def _mm_kernel(x_ref, w_ref, o_ref):
    o_ref[...] = jnp.dot(x_ref[...], w_ref[...], preferred_element_type=F32).astype(o_ref.dtype)


def _matmul(x, w, out_dtype, tm, tn):
    m, k = x.shape
    n = w.shape[1]
    return pl.pallas_call(
        _mm_kernel,
        grid=(m // tm, n // tn),
        in_specs=[pl.BlockSpec((tm, k), lambda i, j: (i, 0)),
                  pl.BlockSpec((k, tn), lambda i, j: (0, j))],
        out_specs=pl.BlockSpec((tm, tn), lambda i, j: (i, j)),
        out_shape=jax.ShapeDtypeStruct((m, n), out_dtype),
        compiler_params=_params("parallel", "arbitrary"),
    )(x, w)


def _silu(x):
    return x * (1.0 / (1.0 + jnp.exp(-x)))


def _ssd_kernel(z_ref, xs_ref, bc_ref, dt_ref, cbuf_ref, h0_ref, cw_ref, cb_ref, dtb_ref,
                alog_ref, dsk_ref, nw_ref, y_ref, hout_ref, cout_ref,
                ext_ref, xbc_ref, st_ref, yacc_ref, *, nvalid):
    l = CHUNK
    c = pl.program_id(1)
    nc = pl.num_programs(1)
    gn = SSD_GROUPS * SSD_STATE

    @pl.when(c == 0)
    def _():
        ext_ref[0:CONV_TAIL, :] = cbuf_ref[0]
        st_ref[...] = h0_ref[0]

    ext_ref[CONV_TAIL:CONV_TAIL + l, 0:SSD_INNER] = xs_ref[...]
    ext_ref[CONV_TAIL:CONV_TAIL + l, SSD_INNER:SSD_CONV_DIM] = bc_ref[...]

    cblk = 512
    for cc in range(SSD_CONV_DIM // cblk):
        sl = slice(cc * cblk, (cc + 1) * cblk)
        acc = cb_ref[:, sl] + ext_ref[CONV_TAIL - 3:CONV_TAIL - 3 + l, sl] * cw_ref[0:1, sl]
        for tap in range(1, SSD_CONV):
            acc = acc + ext_ref[CONV_TAIL - 3 + tap:CONV_TAIL - 3 + tap + l, sl] * cw_ref[tap:tap + 1, sl]
        xbc_ref[:, sl] = _silu(acc)

    @pl.when(c == nc - 1)
    def _():
        cout_ref[0] = ext_ref[nvalid + CONV_TAIL - 3:nvalid + CONV_TAIL, :]

    ext_ref[0:CONV_TAIL, :] = ext_ref[l:l + CONV_TAIL, :]

    row = lax.broadcasted_iota(jnp.int32, (l, LANE), 0)
    lane = lax.broadcasted_iota(jnp.int32, (l, LANE), 1)
    lo = lane < SSD_HEAD_DIM
    causal = row >= lane
    xdt = dt_ref[...] + dtb_ref[...]
    dtv = jnp.maximum(xdt, 0.0) + jnp.log1p(jnp.exp(-jnp.abs(xdt)))
    if nvalid < l:
        dtv = jnp.where(row < nvalid, dtv, 0.0)
    a = dtv * (-jnp.exp(alog_ref[...]))
    tril = causal.astype(F32)
    a_cs = jnp.dot(tril, a, preferred_element_type=F32, precision=lax.Precision.HIGHEST)
    a_cs_t = a_cs.T
    a_last = a_cs[l - 1:l, :]
    exp_acs = jnp.exp(a_cs)
    dte = jnp.exp(a_last - a_cs)
    cdec = jnp.exp(a_last)

    def pick(arr, h0, h1):
        return jnp.where(lo, arr[:, h0:h0 + 1], arr[:, h1:h1 + 1])

    for g in range(SSD_GROUPS):
        bg = xbc_ref[:, SSD_INNER + g * SSD_STATE:SSD_INNER + (g + 1) * SSD_STATE].astype(BF16)
        cg = xbc_ref[:, SSD_INNER + gn + g * SSD_STATE:SSD_INNER + gn + (g + 1) * SSD_STATE].astype(BF16)
        cbm = lax.dot_general(cg, bg, (((1,), (1,)), ((), ())), preferred_element_type=F32)
        pairs_per_group = SSD_HEADS // SSD_GROUPS // 2
        for kk in range(pairs_per_group):
            k = g * pairs_per_group + kk
            h0, h1 = 2 * k, 2 * k + 1
            xs_p = xbc_ref[:, k * LANE:(k + 1) * LANE]
            x_p = xs_p * pick(dtv, h0, h1)
            x_pb = x_p.astype(BF16)
            ys = []
            for h in (h0, h1):
                seg = a_cs[:, h:h + 1] - a_cs_t[h:h + 1, :]
                dec = jnp.where(causal, jnp.exp(jnp.minimum(seg, 0.0)), 0.0)
                w = (cbm * dec).astype(BF16)
                ys.append(jnp.dot(w, x_pb, preferred_element_type=F32))
            y_d = jnp.where(lo, ys[0], ys[1])
            sp = st_ref[k]
            y_o = lax.dot_general(cg, sp.astype(BF16), (((1,), (1,)), ((), ())),
                                  preferred_element_type=F32) * pick(exp_acs, h0, h1)
            xd = (x_p * pick(dte, h0, h1)).astype(BF16)
            upd = lax.dot_general(xd, bg, (((0,), (0,)), ((), ())), preferred_element_type=F32)
            cd = jnp.where(row < SSD_HEAD_DIM, cdec[:, h0:h0 + 1], cdec[:, h1:h1 + 1])
            st_ref[k] = sp * cd + upd
            yacc_ref[:, k * LANE:(k + 1) * LANE] = y_d + y_o + dsk_ref[:, k * LANE:(k + 1) * LANE] * xs_p

    gw = SSD_INNER // SSD_GROUPS
    for g in range(SSD_GROUPS):
        sl = slice(g * gw, (g + 1) * gw)
        yg = yacc_ref[:, sl] * _silu(z_ref[:, sl])
        ms = jnp.mean(yg * yg, axis=-1, keepdims=True)
        y_ref[:, sl] = (yg * lax.rsqrt(ms + LN_EPS) * nw_ref[:, sl]).astype(y_ref.dtype)

    @pl.when(c == nc - 1)
    def _():
        hout_ref[0] = st_ref[...]


def _ssd(z_src, xs_src, bc_src, dt_src, cols, nb, nc, nvalid, cbuf, h0, lp):
    l = CHUNK
    npair = SSD_HEADS // 2
    kern = functools.partial(_ssd_kernel, nvalid=nvalid)
    rows = lambda b, c: b * nc + c
    vec = lambda w: pl.BlockSpec((1, w), lambda b, c: (0, 0))
    return pl.pallas_call(
        kern,
        grid=(nb, nc),
        in_specs=[
            pl.BlockSpec((l, SSD_INNER), lambda b, c: (rows(b, c), cols[0])),
            pl.BlockSpec((l, SSD_INNER), lambda b, c: (rows(b, c), cols[1])),
            pl.BlockSpec((l, 2 * SSD_GROUPS * SSD_STATE), lambda b, c: (rows(b, c), cols[2])),
            pl.BlockSpec((l, LANE), lambda b, c: (rows(b, c), cols[3])),
            pl.BlockSpec((1, CONV_TAIL, SSD_CONV_DIM), lambda b, c: (b, 0, 0)),
            pl.BlockSpec((1, npair, LANE, SSD_STATE), lambda b, c: (b, 0, 0, 0)),
            pl.BlockSpec((SSD_CONV, SSD_CONV_DIM), lambda b, c: (0, 0)),
            vec(SSD_CONV_DIM), vec(LANE), vec(LANE), vec(SSD_INNER), vec(SSD_INNER),
        ],
        out_specs=[
            pl.BlockSpec((l, SSD_INNER), lambda b, c: (rows(b, c), 0)),
            pl.BlockSpec((1, npair, LANE, SSD_STATE), lambda b, c: (b, 0, 0, 0)),
            pl.BlockSpec((1, SSD_CONV - 1, SSD_CONV_DIM), lambda b, c: (b, 0, 0)),
        ],
        out_shape=[
            jax.ShapeDtypeStruct((nb * nc * l, SSD_INNER), BF16),
            jax.ShapeDtypeStruct((nb, npair, LANE, SSD_STATE), F32),
            jax.ShapeDtypeStruct((nb, SSD_CONV - 1, SSD_CONV_DIM), F32),
        ],
        scratch_shapes=[
            pltpu.VMEM((l + CONV_TAIL, SSD_CONV_DIM), F32),
            pltpu.VMEM((l, SSD_CONV_DIM), F32),
            pltpu.VMEM((npair, LANE, SSD_STATE), F32),
            pltpu.VMEM((l, SSD_INNER), F32),
        ],
        compiler_params=_params("parallel", "arbitrary"),
    )(z_src, xs_src, bc_src, dt_src, cbuf, h0, lp["conv_w_t"], lp["conv_b"], lp["dt_bias"],
      lp["a_log"], lp["d_skip"], lp["ssd_norm_w"])


def _pool_kernel(p_ref, buf_ref, w_ref, sc_ref, y_ref, pout_ref, ext_ref, *, nvalid, start):
    l = CHUNK
    c = pl.program_id(1)
    nc = pl.num_programs(1)

    @pl.when(c == 0)
    def _():
        ext_ref[0:POOL_TAIL, :] = buf_ref[0]

    ext_ref[POOL_TAIL:POOL_TAIL + l, :] = p_ref[...]
    pos = start + c * l + lax.broadcasted_iota(jnp.int32, (l, 1), 0)
    for g, w in enumerate(POOL_WINDOWS):
        sl = slice(g * POOL_GROUP_DIM, (g + 1) * POOL_GROUP_DIM)
        cur = ext_ref[POOL_TAIL:POOL_TAIL + l, sl]
        win = cur
        for k in range(1, w):
            win = win + ext_ref[POOL_TAIL - k:POOL_TAIL - k + l, sl]
        cnt = jnp.minimum(pos + 1, w).astype(F32)
        pooled = (win / cnt - cur).astype(BF16)
        mixed = jnp.dot(pooled, w_ref[g], preferred_element_type=F32)
        y_ref[:, sl] = (mixed * sc_ref[:, sl]).astype(y_ref.dtype)

    @pl.when(c == nc - 1)
    def _():
        pout_ref[0] = ext_ref[nvalid + 1:nvalid + 1 + POOL_BUF, :]

    ext_ref[0:POOL_TAIL, :] = ext_ref[l:l + POOL_TAIL, :]


def _pool(p_src, col, nb, nc, nvalid, start, buf, lp):
    l = CHUNK
    kern = functools.partial(_pool_kernel, nvalid=nvalid, start=start)
    return pl.pallas_call(
        kern,
        grid=(nb, nc),
        in_specs=[
            pl.BlockSpec((l, POOL_WIDTH), lambda b, c: (b * nc + c, col)),
            pl.BlockSpec((1, POOL_TAIL, POOL_WIDTH), lambda b, c: (b, 0, 0)),
            pl.BlockSpec((POOL_GROUPS, POOL_GROUP_DIM, POOL_GROUP_DIM), lambda b, c: (0, 0, 0)),
            pl.BlockSpec((1, POOL_WIDTH), lambda b, c: (0, 0)),
        ],
        out_specs=[
            pl.BlockSpec((l, POOL_WIDTH), lambda b, c: (b * nc + c, 0)),
            pl.BlockSpec((1, POOL_BUF, POOL_WIDTH), lambda b, c: (b, 0, 0)),
        ],
        out_shape=[
            jax.ShapeDtypeStruct((nb * nc * l, POOL_WIDTH), BF16),
            jax.ShapeDtypeStruct((nb, POOL_BUF, POOL_WIDTH), F32),
        ],
        scratch_shapes=[pltpu.VMEM((l + POOL_TAIL, POOL_WIDTH), F32)],
        compiler_params=_params("parallel", "arbitrary"),
    )(p_src, buf, lp["pool_w"], lp["pool_scale"])


def _rel_bucket(dist):
    n = jnp.maximum(dist, 0)
    large = REL_MAX_EXACT + (jnp.log(jnp.maximum(n, 1).astype(F32) / REL_MAX_EXACT)
                             / math.log(REL_MAX_DIST / REL_MAX_EXACT)
                             * (REL_BUCKETS - REL_MAX_EXACT)).astype(jnp.int32)
    large = jnp.minimum(large, REL_BUCKETS - 1)
    return jnp.where(n < REL_MAX_EXACT, n, large)


def _moba_prompt_kernel(q_ref, k_ref, v_ref, bias_ref, o_ref, kb_ref, vb_ref, km_ref,
                        m_ref, l_ref, acc_ref, *, nblk):
    blk = MOBA_BLOCK
    qi = pl.program_id(2)
    scale = ATT_HEAD_DIM ** -0.5

    @pl.when(qi == 0)
    def _():
        kb_ref[...] = k_ref[...].astype(BF16)
        vb_ref[...] = v_ref[...].astype(BF16)
        km_ref[...] = jnp.zeros_like(km_ref)
        for n in range(nblk):
            km_ref[n:n + 1, :] = jnp.sum(k_ref[n * blk:(n + 1) * blk, :], axis=0, keepdims=True) / blk

    qb = q_ref[...].astype(BF16)
    nt = (((1,), (1,)), ((), ()))
    gate = lax.dot_general(qb, km_ref[...].astype(BF16), nt, preferred_element_type=F32)
    lane = lax.broadcasted_iota(jnp.int32, gate.shape, 1)
    gate = jnp.where(lane < qi, gate, -jnp.inf)
    rank = jnp.zeros(gate.shape, jnp.int32)
    for s in range(1, nblk):
        rank = rank + (pltpu.roll(gate, s, 1) >= gate).astype(jnp.int32)
        rank = rank + (pltpu.roll(gate, LANE - s, 1) > gate).astype(jnp.int32)
    sel = jnp.where((rank < MOBA_TOPK) & (lane < qi), 1.0, 0.0)

    row2 = lax.broadcasted_iota(jnp.int32, (blk, blk), 0)
    col2 = lax.broadcasted_iota(jnp.int32, (blk, blk), 1)
    own = pl.multiple_of(qi * blk, blk)
    s0 = lax.dot_general(qb, kb_ref[pl.ds(own, blk), :], nt, preferred_element_type=F32) * scale + bias_ref[0, 0]
    s0 = jnp.where(row2 >= col2, s0, -jnp.inf)
    m0 = jnp.max(s0, axis=-1, keepdims=True)
    p0 = jnp.exp(s0 - m0)
    m_ref[...] = m0
    l_ref[...] = jnp.sum(p0, axis=-1, keepdims=True)
    acc_ref[...] = jnp.dot(p0.astype(BF16), vb_ref[pl.ds(own, blk), :], preferred_element_type=F32)

    far = bias_ref[0, 1, blk - 1:blk, 0:1]

    for kj in range(nblk - 1):
        @pl.when(kj < qi)
        def _():
            sc = lax.dot_general(qb, kb_ref[kj * blk:(kj + 1) * blk, :], nt, preferred_element_type=F32) * scale
            bias = jnp.where(kj == qi - 1, bias_ref[0, 1], far)
            sc = jnp.where(sel[:, kj:kj + 1] > 0.0, sc + bias, -jnp.inf)
            m_old = m_ref[...]
            m_new = jnp.maximum(m_old, jnp.max(sc, axis=-1, keepdims=True))
            alpha = jnp.exp(m_old - m_new)
            p = jnp.exp(sc - m_new)
            l_ref[...] = alpha * l_ref[...] + jnp.sum(p, axis=-1, keepdims=True)
            acc_ref[...] = alpha * acc_ref[...] + jnp.dot(
                p.astype(BF16), vb_ref[kj * blk:(kj + 1) * blk, :], preferred_element_type=F32)
            m_ref[...] = m_new

    o_ref[...] = (acc_ref[...] / l_ref[...]).astype(o_ref.dtype)


def _moba_prompt(proj, nb, seq, bias_tiles):
    blk = MOBA_BLOCK
    nblk = seq // blk
    hd = ATT_HEAD_DIM
    qc, kc, vc = R_Q // hd, R_K // hd, R_V // hd
    kern = functools.partial(_moba_prompt_kernel, nblk=nblk)
    return pl.pallas_call(
        kern,
        grid=(nb, ATT_HEADS, nblk),
        in_specs=[
            pl.BlockSpec((blk, hd), lambda b, h, i: (b * nblk + i, qc + h)),
            pl.BlockSpec((seq, hd), lambda b, h, i: (b, kc + h)),
            pl.BlockSpec((seq, hd), lambda b, h, i: (b, vc + h)),
            pl.BlockSpec((1, 2, blk, blk), lambda b, h, i: (h, 0, 0, 0)),
        ],
        out_specs=pl.BlockSpec((blk, hd), lambda b, h, i: (b * nblk + i, h)),
        out_shape=jax.ShapeDtypeStruct((nb * seq, ATT_WIDTH), BF16),
        scratch_shapes=[
            pltpu.VMEM((seq, hd), BF16), pltpu.VMEM((seq, hd), BF16), pltpu.VMEM((LANE, hd), F32),
            pltpu.VMEM((blk, 1), F32), pltpu.VMEM((blk, 1), F32), pltpu.VMEM((blk, hd), F32),
        ],
        compiler_params=_params("parallel", "parallel", "arbitrary"),
    )(proj, proj, proj, bias_tiles)


def _gate_s_kernel(pt_ref, q_ref, k0_ref, k1_ref, e_ref, o_ref):
    n = pl.program_id(1)
    km = (jnp.sum(k0_ref[0, 0], axis=0, keepdims=True)
          + jnp.sum(k1_ref[0, 0], axis=0, keepdims=True)) / MOBA_BLOCK
    prod = q_ref[0].astype(BF16).astype(F32) * km.astype(BF16).astype(F32)
    prod8 = jnp.broadcast_to(prod, (SUBLANE, prod.shape[1]))
    g = jnp.dot(prod8, e_ref[...], preferred_element_type=F32, precision=lax.Precision.HIGHEST)
    o_ref[0, pl.ds(n, 1), :] = g[0:1, :]


def _gate_sample(page_table, q3, cache_k4, layer, head_ind):
    nd, n_pages = page_table.shape
    page = cache_k4.shape[2]
    ppb = MOBA_BLOCK // page
    nblk = n_pages // ppb
    spec_k = lambda off: pl.BlockSpec(
        (1, 1, page, ATT_WIDTH), lambda b, n, pt: (pt[b, ppb * n + off], layer, 0, 0))
    return pl.pallas_call(
        _gate_s_kernel,
        grid_spec=pltpu.PrefetchScalarGridSpec(
            num_scalar_prefetch=1,
            grid=(nd, nblk),
            in_specs=[
                pl.BlockSpec((1, 1, ATT_WIDTH), lambda b, n, pt: (b, 0, 0)),
                spec_k(0), spec_k(1),
                pl.BlockSpec((ATT_WIDTH, LANE), lambda b, n, pt: (0, 0)),
            ],
            out_specs=pl.BlockSpec((1, nblk, LANE), lambda b, n, pt: (b, 0, 0)),
        ),
        out_shape=jax.ShapeDtypeStruct((nd, nblk, LANE), F32),
        compiler_params=_params("parallel", "arbitrary"),
    )(page_table, q3, cache_k4, cache_k4, head_ind)


def _top3_kernel(g_ref, o_ref):
    g = g_ref[0]
    nblk = g.shape[0]
    row = lax.broadcasted_iota(jnp.int32, g.shape, 0)
    picks = []
    for _ in range(MOBA_TOPK):
        m = jnp.max(g, axis=0, keepdims=True)
        idx = jnp.min(jnp.where(g == m, row, nblk), axis=0, keepdims=True)
        g = jnp.where(row == idx, -jnp.inf, g)
        picks.append(idx)
    picks += [jnp.zeros_like(picks[0])] * (SUBLANE - MOBA_TOPK)
    o_ref[0] = jnp.concatenate(picks, axis=0)


def _top3(gates):
    nd, nblk, _ = gates.shape
    return pl.pallas_call(
        _top3_kernel,
        grid=(nd,),
        in_specs=[pl.BlockSpec((1, nblk, LANE), lambda b: (b, 0, 0))],
        out_specs=pl.BlockSpec((1, SUBLANE, LANE), lambda b: (b, 0, 0)),
        out_shape=jax.ShapeDtypeStruct((nd, SUBLANE, LANE), jnp.int32),
        compiler_params=_params("parallel"),
    )(gates)


def _att_s_kernel(pt_ref, sel_ref, q_ref, kn_ref, vn_ref, k0_ref, k1_ref, v0_ref, v1_ref,
                  b0_ref, bl_ref, o_ref, m_ref, l_ref, acc_ref, *, last_blk):
    b, h, j = pl.program_id(0), pl.program_id(1), pl.program_id(2)
    scale = ATT_HEAD_DIM ** -0.5
    nt = (((1,), (1,)), ((), ()))
    q8 = jnp.broadcast_to(q_ref[0], (SUBLANE, ATT_HEAD_DIM)).astype(BF16)

    @pl.when(j == 0)
    def _():
        s_own = jnp.sum(q8[0:1].astype(F32) * kn_ref[0].astype(BF16).astype(F32), axis=-1, keepdims=True)
        m_ref[...] = s_own * scale + b0_ref[0]
        l_ref[...] = jnp.ones_like(l_ref)
        acc_ref[...] = vn_ref[0].astype(BF16).astype(F32)

    kb = jnp.concatenate([k0_ref[0, 0], k1_ref[0, 0]], axis=0).astype(BF16)
    vb = jnp.concatenate([v0_ref[0, 0], v1_ref[0, 0]], axis=0).astype(BF16)
    s = lax.dot_general(q8, kb, nt, preferred_element_type=F32)[0:1, :] * scale
    s = s + jnp.where(sel_ref[b, j, h] == last_blk, bl_ref[0], bl_ref[0, :, 0:1])
    m_old = m_ref[...]
    m_new = jnp.maximum(m_old, jnp.max(s, axis=-1, keepdims=True))
    alpha = jnp.exp(m_old - m_new)
    p = jnp.exp(s - m_new)
    l_new = alpha * l_ref[...] + jnp.sum(p, axis=-1, keepdims=True)
    p8 = jnp.broadcast_to(p, (SUBLANE, p.shape[1])).astype(BF16)
    acc = alpha * acc_ref[...] + jnp.dot(p8, vb, preferred_element_type=F32)[0:1, :]
    m_ref[...] = m_new
    l_ref[...] = l_new
    acc_ref[...] = acc
    o_ref[0] = (acc / l_new).astype(o_ref.dtype)


def _att_sample(page_table, sel, proj3, cache_k4, cache_v4, layer, bias0, bias_last):
    nd, n_pages = page_table.shape
    page = cache_k4.shape[2]
    ppb = MOBA_BLOCK // page
    nblk = n_pages // ppb
    hd = ATT_HEAD_DIM
    qc, kc, vc = R_Q // hd, R_K // hd, R_V // hd
    kern = functools.partial(_att_s_kernel, last_blk=nblk - 1)
    tok = lambda c0: pl.BlockSpec((1, 1, hd), lambda b, h, j, pt, sl: (b, 0, c0 + h))
    pg = lambda off: pl.BlockSpec(
        (1, 1, page, hd), lambda b, h, j, pt, sl: (pt[b, ppb * sl[b, j, h] + off], layer, 0, h))
    return pl.pallas_call(
        kern,
        grid_spec=pltpu.PrefetchScalarGridSpec(
            num_scalar_prefetch=2,
            grid=(nd, ATT_HEADS, MOBA_TOPK),
            in_specs=[
                tok(qc), tok(kc), tok(vc), pg(0), pg(1), pg(0), pg(1),
                pl.BlockSpec((1, 1, 1), lambda b, h, j, pt, sl: (h, 0, 0)),
                pl.BlockSpec((1, 1, MOBA_BLOCK), lambda b, h, j, pt, sl: (h, 0, 0)),
            ],
            out_specs=pl.BlockSpec((1, 1, hd), lambda b, h, j, pt, sl: (b, 0, h)),
            scratch_shapes=[pltpu.VMEM((1, 1), F32), pltpu.VMEM((1, 1), F32), pltpu.VMEM((1, hd), F32)],
        ),
        out_shape=jax.ShapeDtypeStruct((nd, 1, ATT_WIDTH), BF16),
        compiler_params=_params("parallel", "parallel", "arbitrary"),
    )(page_table, sel, proj3, proj3, proj3, cache_k4, cache_k4, cache_v4, cache_v4, bias0, bias_last)


def _merge_kernel(y0_ref, y1_ref, y2_ref, w0_ref, w1_ref, w2_ref, g0_ref, g1_ref, g2_ref, o_ref):
    acc = None
    for y_ref, w_ref, g_ref in ((y0_ref, w0_ref, g0_ref), (y1_ref, w1_ref, g1_ref), (y2_ref, w2_ref, g2_ref)):
        t = jnp.dot(y_ref[...], w_ref[...], preferred_element_type=F32)
        t = t * (1.0 / (1.0 + jnp.exp(-g_ref[...])))
        acc = t if acc is None else acc + t
    o_ref[...] = acc.astype(o_ref.dtype)


def _merge(ys, ws, proj, tm, tn):
    m = ys[0].shape[0]
    kd = ys[0].shape[1]
    gc = R_GATE // tn
    gstep = D_MODEL // tn
    yspec = pl.BlockSpec((tm, kd), lambda i, j: (i, 0))
    wspec = pl.BlockSpec((kd, tn), lambda i, j: (0, j))
    gspec = lambda br: pl.BlockSpec((tm, tn), lambda i, j: (i, gc + br * gstep + j))
    return pl.pallas_call(
        _merge_kernel,
        grid=(m // tm, D_MODEL // tn),
        in_specs=[yspec, yspec, yspec, wspec, wspec, wspec, gspec(0), gspec(1), gspec(2)],
        out_specs=pl.BlockSpec((tm, tn), lambda i, j: (i, j)),
        out_shape=jax.ShapeDtypeStruct((m, D_MODEL), BF16),
        compiler_params=_params("parallel", "arbitrary"),
    )(*ys, *ws, proj, proj, proj)


def _ln(x, g, b):
    mu = jnp.mean(x, axis=-1, keepdims=True)
    xc = x - mu
    var = jnp.mean(xc * xc, axis=-1, keepdims=True)
    return xc * lax.rsqrt(var + LN_EPS) * g + b


def _mm_ln_kernel(a_ref, w_ref, r_ref, g_ref, b_ref, o_ref, ob_ref):
    k = pl.program_id(1)
    part = jnp.dot(a_ref[...], w_ref[...], preferred_element_type=F32)

    @pl.when(k == 0)
    def _():
        o_ref[...] = DEEPNORM_ALPHA * r_ref[...] + part

    @pl.when(k > 0)
    def _():
        o_ref[...] += part

    @pl.when(k == pl.num_programs(1) - 1)
    def _():
        y = _ln(o_ref[...], g_ref[...], b_ref[...])
        o_ref[...] = y
        ob_ref[...] = y.astype(BF16)


def _matmul_ln(a, w, res, g, b, tm, tk):
    m, kd = a.shape
    n = w.shape[1]
    return pl.pallas_call(
        _mm_ln_kernel,
        grid=(m // tm, kd // tk),
        in_specs=[
            pl.BlockSpec((tm, tk), lambda i, k: (i, k)),
            pl.BlockSpec((tk, n), lambda i, k: (k, 0)),
            pl.BlockSpec((tm, n), lambda i, k: (i, 0)),
            pl.BlockSpec((1, n), lambda i, k: (0, 0)),
            pl.BlockSpec((1, n), lambda i, k: (0, 0)),
        ],
        out_specs=[pl.BlockSpec((tm, n), lambda i, k: (i, 0)),
                   pl.BlockSpec((tm, n), lambda i, k: (i, 0))],
        out_shape=[jax.ShapeDtypeStruct((m, n), F32), jax.ShapeDtypeStruct((m, n), BF16)],
        compiler_params=_params("parallel", "arbitrary"),
    )(a, w, res, g, b)


def _add_ln_kernel(r_ref, f_ref, g_ref, b_ref, o_ref, ob_ref):
    y = _ln(DEEPNORM_ALPHA * r_ref[...] + f_ref[...], g_ref[...], b_ref[...])
    o_ref[...] = y
    ob_ref[...] = y.astype(BF16)


def _add_ln(res, f, g, b, tm):
    m, n = res.shape
    row = pl.BlockSpec((tm, n), lambda i: (i, 0))
    vec = pl.BlockSpec((1, n), lambda i: (0, 0))
    return pl.pallas_call(
        _add_ln_kernel,
        grid=(m // tm,),
        in_specs=[row, row, vec, vec],
        out_specs=[row, row],
        out_shape=[jax.ShapeDtypeStruct((m, n), F32), jax.ShapeDtypeStruct((m, n), BF16)],
        compiler_params=_params("parallel"),
    )(res, f, g, b)


def _top_desc(vals, k):
    n = vals.shape[0]
    row = lax.broadcasted_iota(jnp.int32, vals.shape, 0)
    tops = []
    for _ in range(k):
        m = jnp.max(vals, axis=0, keepdims=True)
        idx = jnp.min(jnp.where(vals == m, row, n), axis=0, keepdims=True)
        vals = jnp.where(row == idx, -jnp.inf, vals)
        tops.append(m)
    return tops


ST_TAU, ST_M1, ST_M2, ST_ZINV = 0, PEER_HEADS, 2 * PEER_HEADS, 3 * PEER_HEADS


def _route_kernel(q_ref, sk_ref, s1_ref, s2_ref, st_ref):
    nt = (((1,), (1,)), ((), ()))
    taus, m1s, m2s, zinvs = [], [], [], []
    for h in range(PEER_HEADS):
        st = []
        for c in range(2):
            col = (h * 2 + c) * PEER_HALF
            s = lax.dot_general(sk_ref[h, c], q_ref[:, col:col + PEER_HALF], nt,
                                preferred_element_type=F32)
            (s1_ref if c == 0 else s2_ref)[h * PEER_KEYS:(h + 1) * PEER_KEYS, :] = s
            st.append(s)
        u = _top_desc(st[0], PEER_TOPK)
        v = _top_desc(st[1], PEER_TOPK)
        vcat = jnp.concatenate(v, axis=0)
        cand = jnp.concatenate([ui + vcat for ui in u], axis=0)
        t = _top_desc(cand, PEER_TOPK)
        z = jnp.ones_like(t[0])
        for tk in t[1:]:
            z = z + jnp.exp(tk - t[0])
        taus.append(t[PEER_TOPK - 1])
        m1s.append(u[0])
        m2s.append(v[0])
        zinvs.append(1.0 / z)
    st_ref[...] = jnp.concatenate(taus + m1s + m2s + zinvs, axis=0)


def _route(qp, subkeys, tb):
    t = qp.shape[0]
    w = PEER_HEADS * PEER_KEYS
    col = lambda n: pl.BlockSpec((n, tb), lambda i: (0, i))
    sds = lambda n: jax.ShapeDtypeStruct((n, t), F32)
    return pl.pallas_call(
        _route_kernel,
        grid=(t // tb,),
        in_specs=[pl.BlockSpec((tb, PEER_HEADS * PEER_QDIM), lambda i: (i, 0)),
                  pl.BlockSpec((PEER_HEADS, 2, PEER_KEYS, PEER_HALF), lambda i: (0, 0, 0, 0))],
        out_specs=[col(w), col(w), col(4 * PEER_HEADS)],
        out_shape=[sds(w), sds(w), sds(4 * PEER_HEADS)],
        compiler_params=_params("parallel"),
    )(qp, subkeys)


def _gelu_tanh(x):
    return 0.5 * x * (1.0 + jnp.tanh(math.sqrt(2.0 / math.pi) * (x + 0.044715 * (x * x * x))))


def _peer_kernel(x_ref, u_ref, v_ref, s1_ref, s2_ref, st_ref, o_ref, e2_ref, *, te):
    j = pl.program_id(1)
    na = te // PEER_KEYS
    nt = (((1,), (1,)), ((), ()))

    @pl.when(j == 0)
    def _():
        for h in range(PEER_HEADS):
            hs = slice(h * PEER_KEYS, (h + 1) * PEER_KEYS)
            e2_ref[hs, :] = jnp.exp(s2_ref[hs, :] - st_ref[ST_M2 + h:ST_M2 + h + 1, :])

    hid = lax.dot_general(u_ref[...], x_ref[...], nt, preferred_element_type=F32)
    pieces = []
    for ai in range(na):
        a = j * na + ai
        g = None
        for h in range(PEER_HEADS):
            hs = slice(h * PEER_KEYS, (h + 1) * PEER_KEYS)
            s1row = s1_ref[pl.ds(h * PEER_KEYS + a, 1), :]
            e1row = jnp.exp(s1row - st_ref[ST_M1 + h:ST_M1 + h + 1, :]) * st_ref[ST_ZINV + h:ST_ZINV + h + 1, :]
            keep = (s1row + s2_ref[hs, :]) >= st_ref[ST_TAU + h:ST_TAU + h + 1, :]
            t = jnp.where(keep, e1row * e2_ref[hs, :], 0.0)
            g = t if g is None else g + t
        pieces.append(g * _gelu_tanh(hid[ai * PEER_KEYS:(ai + 1) * PEER_KEYS, :]))
    wgt = jnp.concatenate(pieces, axis=0).astype(BF16)
    part = lax.dot_general(wgt, v_ref[...], (((0,), (0,)), ((), ())), preferred_element_type=F32)

    @pl.when(j == 0)
    def _():
        o_ref[...] = part

    @pl.when(j > 0)
    def _():
        o_ref[...] += part


def _peer_dense(xb, u_tab, v_tab, s1t, s2t, stats, tb, te):
    t, d = xb.shape
    ne = u_tab.shape[0]
    w = PEER_HEADS * PEER_KEYS
    kern = functools.partial(_peer_kernel, te=te)
    once = pl.Buffered(1)
    col = lambda n: pl.BlockSpec((n, tb), lambda i, j: (0, i), pipeline_mode=once)
    return pl.pallas_call(
        kern,
        grid=(t // tb, ne // te),
        in_specs=[
            pl.BlockSpec((tb, d), lambda i, j: (i, 0), pipeline_mode=once),
            pl.BlockSpec((te, d), lambda i, j: (j, 0)),
            pl.BlockSpec((te, d), lambda i, j: (j, 0)),
            col(w), col(w), col(4 * PEER_HEADS),
        ],
        out_specs=pl.BlockSpec((tb, d), lambda i, j: (i, 0)),
        out_shape=jax.ShapeDtypeStruct((t, d), F32),
        scratch_shapes=[pltpu.VMEM((w, tb), F32)],
        compiler_params=_params("parallel", "arbitrary"),
    )(xb, u_tab, v_tab, s1t, s2t, stats)


def _prep_layer(i, w_in, conv_w, conv_b, dt_bias, a_log, d_skip, ssd_norm_w, pool_w, pool_scale,
                w_br_ssd, w_br_pool, w_br_att, w_out, ln1_g, ln1_b, peer_wq, peer_subkeys,
                peer_u, peer_v, ln2_g, ln2_b):
    w = w_in[i]
    xs_end = OFF_XBC + SSD_INNER
    w_r = jnp.concatenate([
        w[:, OFF_Z:OFF_XBC], w[:, OFF_XBC:xs_end], w[:, OFF_POOL:OFF_Q], w[:, OFF_Q:OFF_K],
        w[:, OFF_K:OFF_V], w[:, OFF_V:OFF_GATE], w[:, OFF_GATE:PROJ_DIM], w[:, xs_end:OFF_DT],
        w[:, OFF_DT:OFF_POOL],
        jnp.zeros((D_MODEL, R_DIM - R_DT - SSD_HEADS), w.dtype)], axis=1).astype(BF16)
    padh = lambda v: jnp.pad(v.astype(F32), (0, LANE - SSD_HEADS))[None, :]
    row = lambda v: v.astype(F32)[None, :]
    return {
        "w_in": w_r,
        "conv_w_t": conv_w[i].astype(F32).T,
        "conv_b": row(conv_b[i]),
        "dt_bias": padh(dt_bias[i]),
        "a_log": padh(a_log[i]),
        "d_skip": row(jnp.repeat(d_skip[i], SSD_HEAD_DIM)),
        "ssd_norm_w": row(ssd_norm_w[i]),
        "pool_w": pool_w[i].astype(BF16),
        "pool_scale": row(pool_scale[i]),
        "w_br": (w_br_ssd[i].astype(BF16), w_br_pool[i].astype(BF16), w_br_att[i].astype(BF16)),
        "w_out": w_out[i].astype(BF16),
        "ln1_g": row(ln1_g[i]), "ln1_b": row(ln1_b[i]),
        "peer_wq": peer_wq[i].astype(BF16),
        "peer_subkeys": peer_subkeys[i].astype(BF16),
        "peer_u": peer_u[i].astype(BF16),
        "peer_v": peer_v[i].astype(BF16),
        "ln2_g": row(ln2_g[i]), "ln2_b": row(ln2_b[i]),
    }


def _tail(x, lp, tm, tb, te):
    ys, proj, x_f32 = x
    merged = _merge(ys, lp["w_br"], proj, tm, 512)
    x1, x1b = _matmul_ln(merged, lp["w_out"], x_f32, lp["ln1_g"], lp["ln1_b"], min(tm, 256), 1024)
    qp = _matmul(x1b, lp["peer_wq"], BF16, tm, 512)
    s1t, s2t, stats = _route(qp, lp["peer_subkeys"], min(tm, 256))
    ffn = _peer_dense(x1b, lp["peer_u"], lp["peer_v"], s1t, s2t, stats, tb, te)
    return _add_ln(x1, ffn, lp["ln2_g"], lp["ln2_b"], min(tm, 256))


def kernel(x_prompt, x_sample, cache_k, cache_v, state_ssm, state_conv, state_pool, page_table, w_in, conv_w, conv_b, dt_bias, a_log, d_skip, ssd_norm_w, pool_w, pool_scale, rel_bias, w_br_ssd, w_br_pool, w_br_att, w_out, ln1_g, ln1_b, peer_wq, peer_subkeys, peer_u, peer_v, ln2_g, ln2_b):
    bp, seq, _ = x_prompt.shape
    nd = x_sample.shape[0]
    n_pool, _, page, _, _ = cache_k.shape
    n_pages = page_table.shape[1]
    past_len = n_pages * page
    npair = SSD_HEADS // 2
    tp = bp * seq
    ts = LANE
    assert seq % MOBA_BLOCK == 0 and seq % CHUNK == 0 and nd <= ts
    assert past_len % MOBA_BLOCK == 0 and MOBA_BLOCK % page == 0

    rb = rel_bias.astype(F32)
    ii = jnp.arange(MOBA_BLOCK)
    d0 = ii[:, None] - ii[None, :]
    bias_tiles = jnp.stack([rb[_rel_bucket(d0)], rb[_rel_bucket(d0 + MOBA_BLOCK)]], axis=0)
    bias_tiles = jnp.transpose(bias_tiles, (3, 0, 1, 2))
    bias0 = rb[_rel_bucket(jnp.zeros((1,), jnp.int32))].T[:, :, None]
    bias_last = rb[_rel_bucket(MOBA_BLOCK - ii)].T[:, None, :]
    head_ind = (jnp.arange(ATT_WIDTH)[:, None] // ATT_HEAD_DIM == jnp.arange(LANE)[None, :]).astype(F32)

    cache_k4 = cache_k.reshape(n_pool, DEPTH, page, ATT_WIDTH)
    cache_v4 = cache_v.reshape(n_pool, DEPTH, page, ATT_WIDTH)

    xp = x_prompt.reshape(tp, D_MODEL)
    xpb = xp.astype(BF16)
    xs = jnp.pad(x_sample.reshape(nd, D_MODEL), ((0, ts - nd), (0, 0)))
    xsb = xs.astype(BF16)

    zero_conv = jnp.zeros((bp, CONV_TAIL, SSD_CONV_DIM), F32)
    zero_ssm = jnp.zeros((bp, npair, LANE, SSD_STATE), F32)
    zero_pool = jnp.zeros((bp, POOL_TAIL, POOL_WIDTH), F32)

    outs = {k: [] for k in ("kp", "vp", "ks", "vs", "hp", "hs", "cp", "cs", "pp", "ps")}
    for i in range(DEPTH):
        lp = _prep_layer(i, w_in, conv_w, conv_b, dt_bias, a_log, d_skip, ssd_norm_w, pool_w, pool_scale,
                         w_br_ssd, w_br_pool, w_br_att, w_out, ln1_g, ln1_b, peer_wq, peer_subkeys,
                         peer_u, peer_v, ln2_g, ln2_b)

        proj = _matmul(xpb, lp["w_in"], F32, 512, PROJ_TN)
        nc = seq // CHUNK
        y_ssd, h_new, c_new = _ssd(proj, proj, proj, proj,
                                   (R_Z // SSD_INNER, R_XS // SSD_INNER, R_BC // 1024, R_DT // LANE),
                                   bp, nc, CHUNK, zero_conv, zero_ssm, lp)
        y_pool, p_new = _pool(proj, R_POOL // POOL_WIDTH, bp, nc, CHUNK, 0, zero_pool, lp)
        y_att = _moba_prompt(proj, bp, seq, bias_tiles)
        xp, xpb = _tail(((y_ssd, y_pool, y_att), proj, xp), lp, 512, 512, 512)
        outs["kp"].append(proj[:, R_K:R_K + ATT_WIDTH].reshape(bp, seq, ATT_HEADS, ATT_HEAD_DIM))
        outs["vp"].append(proj[:, R_V:R_V + ATT_WIDTH].reshape(bp, seq, ATT_HEADS, ATT_HEAD_DIM))
        outs["hp"].append(h_new.reshape(bp, SSD_HEADS, SSD_HEAD_DIM, SSD_STATE))
        outs["cp"].append(c_new)
        outs["pp"].append(p_new)

        proj_s = _matmul(xsb, lp["w_in"], F32, ts, PROJ_TN)
        ps = proj_s[:nd]
        chunked = lambda a: jnp.pad(a[:, None, :], ((0, 0), (0, CHUNK - 1), (0, 0))).reshape(nd * CHUNK, a.shape[1])
        cbuf = jnp.pad(state_conv[i].astype(F32), ((0, 0), (CONV_TAIL - (SSD_CONV - 1), 0), (0, 0)))
        pbuf = jnp.pad(state_pool[i].astype(F32), ((0, 0), (POOL_TAIL - POOL_BUF, 0), (0, 0)))
        y_ssd_s, h_new_s, c_new_s = _ssd(
            chunked(ps[:, R_Z:R_Z + SSD_INNER]), chunked(ps[:, R_XS:R_XS + SSD_INNER]),
            chunked(ps[:, R_BC:R_DT]), chunked(ps[:, R_DT:R_END]), (0, 0, 0, 0),
            nd, 1, 1, cbuf, state_ssm[i].astype(F32).reshape(nd, npair, LANE, SSD_STATE), lp)
        y_pool_s, p_new_s = _pool(chunked(ps[:, R_POOL:R_POOL + POOL_WIDTH]), 0, nd, 1, 1, past_len, pbuf, lp)
        proj3 = ps.reshape(nd, 1, R_DIM)
        gates = _gate_sample(page_table, proj3[:, :, R_Q:R_Q + ATT_WIDTH], cache_k4, i, head_ind)
        sel = _top3(gates)
        y_att_s = _att_sample(page_table, sel, proj3, cache_k4, cache_v4, i, bias0, bias_last)
        first = lambda y: jnp.pad(y.reshape(nd, CHUNK, -1)[:, 0], ((0, ts - nd), (0, 0)))
        ys_s = (first(y_ssd_s), first(y_pool_s), jnp.pad(y_att_s.reshape(nd, ATT_WIDTH), ((0, ts - nd), (0, 0))))
        xs, xsb = _tail((ys_s, proj_s, xs), lp, ts, ts, 512)
        outs["ks"].append(ps[:, R_K:R_K + ATT_WIDTH].reshape(nd, 1, ATT_HEADS, ATT_HEAD_DIM))
        outs["vs"].append(ps[:, R_V:R_V + ATT_WIDTH].reshape(nd, 1, ATT_HEADS, ATT_HEAD_DIM))
        outs["hs"].append(h_new_s.reshape(nd, SSD_HEADS, SSD_HEAD_DIM, SSD_STATE))
        outs["cs"].append(c_new_s)
        outs["ps"].append(p_new_s)

    return (xp.reshape(bp, seq, D_MODEL), xs[:nd].reshape(nd, 1, D_MODEL),
            jnp.stack(outs["kp"], axis=1), jnp.stack(outs["vp"], axis=1),
            jnp.stack(outs["ks"], axis=1), jnp.stack(outs["vs"], axis=1),
            jnp.stack(outs["hp"], axis=0), jnp.stack(outs["hs"], axis=0),
            jnp.stack(outs["cp"], axis=0), jnp.stack(outs["cs"], axis=0),
            jnp.stack(outs["pp"], axis=0), jnp.stack(outs["ps"], axis=0))
```

```python
import functools
import math

import jax
import jax.numpy as jnp
from jax import lax
from jax.experimental import pallas as pl
from jax.experimental.pallas import tpu as pltpu

F32 = jnp.float32
BF16 = jnp.bfloat16

D_MODEL = 4096
DEPTH = 2
SSD_HEADS = 32
SSD_HEAD_DIM = 64
SSD_INNER = SSD_HEADS * SSD_HEAD_DIM
SSD_GROUPS = 4
SSD_STATE = 128
SSD_CONV = 4
SSD_CONV_DIM = SSD_INNER + 2 * SSD_GROUPS * SSD_STATE
POOL_WINDOWS = (2, 4, 8, 16)
POOL_GROUPS = 4
POOL_WIDTH = 2048
POOL_GROUP_DIM = POOL_WIDTH // POOL_GROUPS
POOL_BUF = max(POOL_WINDOWS) - 1
ATT_HEADS = 16
ATT_HEAD_DIM = 128
ATT_WIDTH = ATT_HEADS * ATT_HEAD_DIM
MOBA_BLOCK = 256
MOBA_TOPK = 3
REL_BUCKETS = 32
REL_MAX_EXACT = 16
REL_MAX_DIST = 128
PEER_HEADS = 8
PEER_KEYS = 128
PEER_EXPERTS = PEER_KEYS * PEER_KEYS
PEER_QDIM = 256
PEER_HALF = PEER_QDIM // 2
PEER_TOPK = 16
DEEPNORM_ALPHA = (2 * DEPTH) ** 0.25
LN_EPS = 1e-5

OFF_Z = 0
OFF_XBC = OFF_Z + SSD_INNER
OFF_DT = OFF_XBC + SSD_CONV_DIM
OFF_POOL = OFF_DT + SSD_HEADS
OFF_Q = OFF_POOL + POOL_WIDTH
OFF_K = OFF_Q + ATT_WIDTH
OFF_V = OFF_K + ATT_WIDTH
OFF_GATE = OFF_V + ATT_WIDTH
PROJ_DIM = OFF_GATE + 3 * D_MODEL

LANE = 128
SUBLANE = 8
VMEM_LIMIT = 56 * 1024 * 1024

R_Z = 0
R_XS = 2048
R_POOL = 4096
R_Q = 6144
R_K = 8192
R_V = 10240
R_GATE = 12288
R_BC = R_GATE + 3 * D_MODEL
R_DT = R_BC + 2 * SSD_GROUPS * SSD_STATE
R_END = R_DT + LANE
PROJ_TN = 512
R_DIM = -(-R_END // PROJ_TN) * PROJ_TN

CHUNK = 128
CONV_TAIL = SUBLANE
POOL_TAIL = 16
GATE_BLOCKS_PER_STEP = 4


def _params(*sem):
    return pltpu.CompilerParams(dimension_semantics=sem, vmem_limit_bytes=VMEM_LIMIT)


def _mm_kernel(x_ref, w_ref, o_ref):
    o_ref[...] = jnp.dot(x_ref[...], w_ref[...], preferred_element_type=F32).astype(o_ref.dtype)


def _matmul(x, w, out_dtype, tm, tn):
    m, k = x.shape
    n = w.shape[1]
    return pl.pallas_call(
        _mm_kernel,
        grid=(m // tm, n // tn),
        in_specs=[pl.BlockSpec((tm, k), lambda i, j: (i, 0)),
                  pl.BlockSpec((k, tn), lambda i, j: (0, j))],
        out_specs=pl.BlockSpec((tm, tn), lambda i, j: (i, j)),
        out_shape=jax.ShapeDtypeStruct((m, n), out_dtype),
        compiler_params=_params("parallel", "arbitrary"),
    )(x, w)


def _silu(x):
    return x * (1.0 / (1.0 + jnp.exp(-x)))


def _ssd_kernel(z_ref, xs_ref, bc_ref, dt_ref, cbuf_ref, h0_ref, cw_ref, cb_ref, dtb_ref,
                alog_ref, dsk_ref, nw_ref, y_ref, hout_ref, cout_ref,
                ext_ref, xbc_ref, st_ref, yacc_ref, *, nvalid):
    l = CHUNK
    c = pl.program_id(1)
    nc = pl.num_programs(1)
    gn = SSD_GROUPS * SSD_STATE

    @pl.when(c == 0)
    def _():
        ext_ref[0:CONV_TAIL, :] = cbuf_ref[0]
        st_ref[...] = h0_ref[0]

    ext_ref[CONV_TAIL:CONV_TAIL + l, 0:SSD_INNER] = xs_ref[...]
    ext_ref[CONV_TAIL:CONV_TAIL + l, SSD_INNER:SSD_CONV_DIM] = bc_ref[...]

    cblk = 512
    for cc in range(SSD_CONV_DIM // cblk):
        sl = slice(cc * cblk, (cc + 1) * cblk)
        acc = cb_ref[:, sl] + ext_ref[CONV_TAIL - 3:CONV_TAIL - 3 + l, sl] * cw_ref[0:1, sl]
        for tap in range(1, SSD_CONV):
            acc = acc + ext_ref[CONV_TAIL - 3 + tap:CONV_TAIL - 3 + tap + l, sl] * cw_ref[tap:tap + 1, sl]
        xbc_ref[:, sl] = _silu(acc)

    @pl.when(c == nc - 1)
    def _():
        cout_ref[0] = ext_ref[nvalid + CONV_TAIL - 3:nvalid + CONV_TAIL, :]

    ext_ref[0:CONV_TAIL, :] = ext_ref[l:l + CONV_TAIL, :]

    row = lax.broadcasted_iota(jnp.int32, (l, LANE), 0)
    lane = lax.broadcasted_iota(jnp.int32, (l, LANE), 1)
    lo = lane < SSD_HEAD_DIM
    causal = row >= lane
    xdt = dt_ref[...] + dtb_ref[...]
    dtv = jnp.maximum(xdt, 0.0) + jnp.log1p(jnp.exp(-jnp.abs(xdt)))
    if nvalid < l:
        dtv = jnp.where(row < nvalid, dtv, 0.0)
    a = dtv * (-jnp.exp(alog_ref[...]))
    tril = causal.astype(F32)
    a_cs = jnp.dot(tril, a, preferred_element_type=F32, precision=lax.Precision.HIGHEST)
    a_cs_t = a_cs.T
    a_last = a_cs[l - 1:l, :]
    exp_acs = jnp.exp(a_cs)
    dte = jnp.exp(a_last - a_cs)
    cdec = jnp.exp(a_last)

    def pick(arr, h0, h1):
        return jnp.where(lo, arr[:, h0:h0 + 1], arr[:, h1:h1 + 1])

    for g in range(SSD_GROUPS):
        bg = xbc_ref[:, SSD_INNER + g * SSD_STATE:SSD_INNER + (g + 1) * SSD_STATE].astype(BF16)
        cg = xbc_ref[:, SSD_INNER + gn + g * SSD_STATE:SSD_INNER + gn + (g + 1) * SSD_STATE].astype(BF16)
        cbm = lax.dot_general(cg, bg, (((1,), (1,)), ((), ())), preferred_element_type=F32)
        pairs_per_group = SSD_HEADS // SSD_GROUPS // 2
        for kk in range(pairs_per_group):
            k = g * pairs_per_group + kk
            h0, h1 = 2 * k, 2 * k + 1
            xs_p = xbc_ref[:, k * LANE:(k + 1) * LANE]
            x_p = xs_p * pick(dtv, h0, h1)
            x_pb = x_p.astype(BF16)
            ys = []
            for h in (h0, h1):
                seg = a_cs[:, h:h + 1] - a_cs_t[h:h + 1, :]
                dec = jnp.where(causal, jnp.exp(jnp.minimum(seg, 0.0)), 0.0)
                w = (cbm * dec).astype(BF16)
                ys.append(jnp.dot(w, x_pb, preferred_element_type=F32))
            y_d = jnp.where(lo, ys[0], ys[1])
            sp = st_ref[k]
            y_o = lax.dot_general(cg, sp.astype(BF16), (((1,), (1,)), ((), ())),
                                  preferred_element_type=F32) * pick(exp_acs, h0, h1)
            xd = (x_p * pick(dte, h0, h1)).astype(BF16)
            upd = lax.dot_general(xd, bg, (((0,), (0,)), ((), ())), preferred_element_type=F32)
            cd = jnp.where(row < SSD_HEAD_DIM, cdec[:, h0:h0 + 1], cdec[:, h1:h1 + 1])
            st_ref[k] = sp * cd + upd
            yacc_ref[:, k * LANE:(k + 1) * LANE] = y_d + y_o + dsk_ref[:, k * LANE:(k + 1) * LANE] * xs_p

    gw = SSD_INNER // SSD_GROUPS
    for g in range(SSD_GROUPS):
        sl = slice(g * gw, (g + 1) * gw)
        yg = yacc_ref[:, sl] * _silu(z_ref[:, sl])
        ms = jnp.mean(yg * yg, axis=-1, keepdims=True)
        y_ref[:, sl] = (yg * lax.rsqrt(ms + LN_EPS) * nw_ref[:, sl]).astype(y_ref.dtype)

    @pl.when(c == nc - 1)
    def _():
        hout_ref[0] = st_ref[...]


def _ssd(z_src, xs_src, bc_src, dt_src, cols, nb, nc, nvalid, cbuf, h0, lp):
    l = CHUNK
    npair = SSD_HEADS // 2
    kern = functools.partial(_ssd_kernel, nvalid=nvalid)
    rows = lambda b, c: b * nc + c
    vec = lambda w: pl.BlockSpec((1, w), lambda b, c: (0, 0))
    return pl.pallas_call(
        kern,
        grid=(nb, nc),
        in_specs=[
            pl.BlockSpec((l, SSD_INNER), lambda b, c: (rows(b, c), cols[0])),
            pl.BlockSpec((l, SSD_INNER), lambda b, c: (rows(b, c), cols[1])),
            pl.BlockSpec((l, 2 * SSD_GROUPS * SSD_STATE), lambda b, c: (rows(b, c), cols[2])),
            pl.BlockSpec((l, LANE), lambda b, c: (rows(b, c), cols[3])),
            pl.BlockSpec((1, CONV_TAIL, SSD_CONV_DIM), lambda b, c: (b, 0, 0)),
            pl.BlockSpec((1, npair, LANE, SSD_STATE), lambda b, c: (b, 0, 0, 0)),
            pl.BlockSpec((SSD_CONV, SSD_CONV_DIM), lambda b, c: (0, 0)),
            vec(SSD_CONV_DIM), vec(LANE), vec(LANE), vec(SSD_INNER), vec(SSD_INNER),
        ],
        out_specs=[
            pl.BlockSpec((l, SSD_INNER), lambda b, c: (rows(b, c), 0)),
            pl.BlockSpec((1, npair, LANE, SSD_STATE), lambda b, c: (b, 0, 0, 0)),
            pl.BlockSpec((1, SSD_CONV - 1, SSD_CONV_DIM), lambda b, c: (b, 0, 0)),
        ],
        out_shape=[
            jax.ShapeDtypeStruct((nb * nc * l, SSD_INNER), BF16),
            jax.ShapeDtypeStruct((nb, npair, LANE, SSD_STATE), F32),
            jax.ShapeDtypeStruct((nb, SSD_CONV - 1, SSD_CONV_DIM), F32),
        ],
        scratch_shapes=[
            pltpu.VMEM((l + CONV_TAIL, SSD_CONV_DIM), F32),
            pltpu.VMEM((l, SSD_CONV_DIM), F32),
            pltpu.VMEM((npair, LANE, SSD_STATE), F32),
            pltpu.VMEM((l, SSD_INNER), F32),
        ],
        compiler_params=_params("parallel", "arbitrary"),
    )(z_src, xs_src, bc_src, dt_src, cbuf, h0, lp["conv_w_t"], lp["conv_b"], lp["dt_bias"],
      lp["a_log"], lp["d_skip"], lp["ssd_norm_w"])


def _pool_kernel(p_ref, buf_ref, w_ref, sc_ref, y_ref, pout_ref, ext_ref, *, nvalid, start):
    l = CHUNK
    c = pl.program_id(1)
    nc = pl.num_programs(1)

    @pl.when(c == 0)
    def _():
        ext_ref[0:POOL_TAIL, :] = buf_ref[0]

    ext_ref[POOL_TAIL:POOL_TAIL + l, :] = p_ref[...]
    pos = start + c * l + lax.broadcasted_iota(jnp.int32, (l, 1), 0)
    for g, w in enumerate(POOL_WINDOWS):
        sl = slice(g * POOL_GROUP_DIM, (g + 1) * POOL_GROUP_DIM)
        cur = ext_ref[POOL_TAIL:POOL_TAIL + l, sl]
        win = cur
        for k in range(1, w):
            win = win + ext_ref[POOL_TAIL - k:POOL_TAIL - k + l, sl]
        cnt = jnp.minimum(pos + 1, w).astype(F32)
        pooled = (win / cnt - cur).astype(BF16)
        mixed = jnp.dot(pooled, w_ref[g], preferred_element_type=F32)
        y_ref[:, sl] = (mixed * sc_ref[:, sl]).astype(y_ref.dtype)

    @pl.when(c == nc - 1)
    def _():
        pout_ref[0] = ext_ref[nvalid + 1:nvalid + 1 + POOL_BUF, :]

    ext_ref[0:POOL_TAIL, :] = ext_ref[l:l + POOL_TAIL, :]


def _pool(p_src, col, nb, nc, nvalid, start, buf, lp):
    l = CHUNK
    kern = functools.partial(_pool_kernel, nvalid=nvalid, start=start)
    return pl.pallas_call(
        kern,
        grid=(nb, nc),
        in_specs=[
            pl.BlockSpec((l, POOL_WIDTH), lambda b, c: (b * nc + c, col)),
            pl.BlockSpec((1, POOL_TAIL, POOL_WIDTH), lambda b, c: (b, 0, 0)),
            pl.BlockSpec((POOL_GROUPS, POOL_GROUP_DIM, POOL_GROUP_DIM), lambda b, c: (0, 0, 0)),
            pl.BlockSpec((1, POOL_WIDTH), lambda b, c: (0, 0)),
        ],
        out_specs=[
            pl.BlockSpec((l, POOL_WIDTH), lambda b, c: (b * nc + c, 0)),
            pl.BlockSpec((1, POOL_BUF, POOL_WIDTH), lambda b, c: (b, 0, 0)),
        ],
        out_shape=[
            jax.ShapeDtypeStruct((nb * nc * l, POOL_WIDTH), BF16),
            jax.ShapeDtypeStruct((nb, POOL_BUF, POOL_WIDTH), F32),
        ],
        scratch_shapes=[pltpu.VMEM((l + POOL_TAIL, POOL_WIDTH), F32)],
        compiler_params=_params("parallel", "arbitrary"),
    )(p_src, buf, lp["pool_w"], lp["pool_scale"])


def _rel_bucket(dist):
    n = jnp.maximum(dist, 0)
    large = REL_MAX_EXACT + (jnp.log(jnp.maximum(n, 1).astype(F32) / REL_MAX_EXACT)
                             / math.log(REL_MAX_DIST / REL_MAX_EXACT)
                             * (REL_BUCKETS - REL_MAX_EXACT)).astype(jnp.int32)
    large = jnp.minimum(large, REL_BUCKETS - 1)
    return jnp.where(n < REL_MAX_EXACT, n, large)


def _moba_prompt_kernel(q_ref, k_ref, v_ref, bias_ref, o_ref, kb_ref, vb_ref, km_ref, *, nblk):
    blk = MOBA_BLOCK
    qi = pl.program_id(2)
    scale = ATT_HEAD_DIM ** -0.5
    nt = (((1,), (1,)), ((), ()))

    @pl.when(qi == 0)
    def _():
        kb_ref[...] = k_ref[...].astype(BF16)
        vb_ref[...] = v_ref[...].astype(BF16)
        km_ref[...] = jnp.zeros_like(km_ref)
        for n in range(nblk):
            km_ref[n:n + 1, :] = jnp.sum(k_ref[n * blk:(n + 1) * blk, :], axis=0, keepdims=True) / blk

    qb = q_ref[...].astype(BF16)
    gate = lax.dot_general(qb, km_ref[...].astype(BF16), nt, preferred_element_type=F32)
    lane = lax.broadcasted_iota(jnp.int32, gate.shape, 1)
    gate = jnp.where(lane < qi, gate, -jnp.inf)
    rank = jnp.zeros(gate.shape, jnp.int32)
    for s in range(1, nblk):
        rank = rank + (pltpu.roll(gate, s, 1) >= gate).astype(jnp.int32)
        rank = rank + (pltpu.roll(gate, LANE - s, 1) > gate).astype(jnp.int32)
    sel = jnp.where((rank < MOBA_TOPK) & (lane < qi), 1.0, 0.0)

    row2 = lax.broadcasted_iota(jnp.int32, (blk, blk), 0)
    col2 = lax.broadcasted_iota(jnp.int32, (blk, blk), 1)
    causal = row2 >= col2
    far = bias_ref[0, 1, blk - 1:blk, 0:1]

    for k in range(nblk):
        @pl.when(qi == k)
        def _():
            nk = k + 1
            s_all = lax.dot_general(qb, kb_ref[0:nk * blk, :], nt, preferred_element_type=F32) * scale
            pieces = []
            for kj in range(nk):
                sj = s_all[:, kj * blk:(kj + 1) * blk]
                if kj == k:
                    sj = jnp.where(causal, sj + bias_ref[0, 0], -jnp.inf)
                else:
                    bias = bias_ref[0, 1] if kj == k - 1 else far
                    sj = jnp.where(sel[:, kj:kj + 1] > 0.0, sj + bias, -jnp.inf)
                pieces.append(sj)
            mm = pieces[0]
            for pj in pieces[1:]:
                mm = jnp.maximum(mm, pj)
            m = jnp.max(mm, axis=-1, keepdims=True)
            ps = [jnp.exp(pj - m) for pj in pieces]
            tot = ps[0]
            for pj in ps[1:]:
                tot = tot + pj
            l = jnp.sum(tot, axis=-1, keepdims=True)
            p_all = jnp.concatenate(ps, axis=1).astype(BF16) if nk > 1 else ps[0].astype(BF16)
            acc = jnp.dot(p_all, vb_ref[0:nk * blk, :], preferred_element_type=F32)
            o_ref[...] = (acc / l).astype(o_ref.dtype)


def _moba_prompt(proj, nb, seq, bias_tiles):
    blk = MOBA_BLOCK
    nblk = seq // blk
    hd = ATT_HEAD_DIM
    qc, kc, vc = R_Q // hd, R_K // hd, R_V // hd
    kern = functools.partial(_moba_prompt_kernel, nblk=nblk)
    return pl.pallas_call(
        kern,
        grid=(nb, ATT_HEADS, nblk),
        in_specs=[
            pl.BlockSpec((blk, hd), lambda b, h, i: (b * nblk + i, qc + h)),
            pl.BlockSpec((seq, hd), lambda b, h, i: (b, kc + h)),
            pl.BlockSpec((seq, hd), lambda b, h, i: (b, vc + h)),
            pl.BlockSpec((1, 2, blk, blk), lambda b, h, i: (h, 0, 0, 0)),
        ],
        out_specs=pl.BlockSpec((blk, hd), lambda b, h, i: (b * nblk + i, h)),
        out_shape=jax.ShapeDtypeStruct((nb * seq, ATT_WIDTH), BF16),
        scratch_shapes=[
            pltpu.VMEM((seq, hd), BF16), pltpu.VMEM((seq, hd), BF16), pltpu.VMEM((LANE, hd), F32),
        ],
        compiler_params=_params("parallel", "parallel", "arbitrary"),
    )(proj, proj, proj, bias_tiles)


def _gate_s_kernel(pt_ref, q_ref, *refs, nblk, page):
    k_refs, (g_ref, sel_ref) = refs[:-2], refs[-2:]
    n = pl.program_id(1)
    ppb = MOBA_BLOCK // page
    lane = lax.broadcasted_iota(jnp.int32, (ATT_HEADS, LANE), 1)

    @pl.when(n == 0)
    def _():
        g_ref[...] = jnp.full(g_ref.shape, -jnp.inf, F32)

    qf = q_ref[0].astype(BF16).astype(F32)
    g = g_ref[0]
    for bi in range(GATE_BLOCKS_PER_STEP):
        tot = None
        for pg in range(ppb):
            kp = k_refs[bi * ppb + pg][0, 0].reshape(page, ATT_HEADS, ATT_HEAD_DIM)
            part = jnp.sum(kp, axis=0)
            tot = part if tot is None else tot + part
        km = (tot / MOBA_BLOCK).astype(BF16).astype(F32)
        val = jnp.sum(qf * km, axis=-1, keepdims=True)
        g = jnp.where(lane == n * GATE_BLOCKS_PER_STEP + bi, val, g)
    g_ref[0] = g

    @pl.when(n == pl.num_programs(1) - 1)
    def _():
        gg = g
        sel = jnp.zeros((ATT_HEADS, LANE), jnp.int32)
        for j in range(MOBA_TOPK):
            m = jnp.max(gg, axis=-1, keepdims=True)
            idx = jnp.min(jnp.where(gg == m, lane, LANE), axis=-1, keepdims=True)
            gg = jnp.where(lane == idx, -jnp.inf, gg)
            sel = jnp.where(lane == j, idx, sel)
        sel_ref[0] = sel


def _gate_sample(page_table, q3, cache_rows, layer):
    nd, n_pages = page_table.shape
    page = cache_rows.shape[2] // ATT_HEADS
    ppb = MOBA_BLOCK // page
    nblk = n_pages // ppb
    gbs = GATE_BLOCKS_PER_STEP
    assert nblk % gbs == 0 and nblk <= LANE

    def spec_k(off):
        return pl.BlockSpec((1, 1, page * ATT_HEADS, ATT_HEAD_DIM),
                            lambda b, n, pt: (pt[b, ppb * gbs * n + off], layer, 0, 0))

    kern = functools.partial(_gate_s_kernel, nblk=nblk, page=page)
    out = pl.BlockSpec((1, ATT_HEADS, LANE), lambda b, n, pt: (b, 0, 0))
    return pl.pallas_call(
        kern,
        grid_spec=pltpu.PrefetchScalarGridSpec(
            num_scalar_prefetch=1,
            grid=(nd, nblk // gbs),
            in_specs=[pl.BlockSpec((1, ATT_HEADS, ATT_HEAD_DIM), lambda b, n, pt: (b, 0, 0))]
            + [spec_k(off) for off in range(ppb * gbs)],
            out_specs=[out, out],
        ),
        out_shape=[jax.ShapeDtypeStruct((nd, ATT_HEADS, LANE), F32),
                   jax.ShapeDtypeStruct((nd, ATT_HEADS, LANE), jnp.int32)],
        compiler_params=_params("parallel", "arbitrary"),
    )(page_table, q3, *([cache_rows] * (ppb * gbs)))


def _att_s_kernel(pt_ref, sel_ref, q_ref, kn_ref, vn_ref, b0_ref, bl_ref, ck_hbm, cv_hbm, o_ref,
                  kbuf, vbuf, sem, *, layer, last_blk, page):
    s = pl.program_id(0)
    ns = pl.num_programs(0)
    ppb = MOBA_BLOCK // page
    scale = ATT_HEAD_DIM ** -0.5

    def copies(step, slot):
        b = step // ATT_HEADS
        h = step % ATT_HEADS
        out = []
        for j in range(MOBA_TOPK):
            blk = sel_ref[b, h, j]
            for pg in range(ppb):
                pidx = pt_ref[b, ppb * blk + pg]
                dst = pl.ds(pg * page, page)
                out.append(pltpu.make_async_copy(ck_hbm.at[pidx, layer, :, h, :],
                                                 kbuf.at[slot, j, dst, :], sem.at[slot, 0, j, pg]))
                out.append(pltpu.make_async_copy(cv_hbm.at[pidx, layer, :, h, :],
                                                 vbuf.at[slot, j, dst, :], sem.at[slot, 1, j, pg]))
        return out

    @pl.when(s == 0)
    def _():
        for c in copies(0, 0):
            c.start()

    @pl.when(s + 1 < ns)
    def _():
        for c in copies(s + 1, (s + 1) % 2):
            c.start()

    slot = s % 2
    for c in copies(s, slot):
        c.wait()

    b = s // ATT_HEADS
    h = s % ATT_HEADS
    nt = (((1,), (1,)), ((), ()))
    nk = MOBA_TOPK * MOBA_BLOCK
    q8 = jnp.broadcast_to(q_ref[0], (SUBLANE, ATT_HEAD_DIM)).astype(BF16)
    kb = kbuf[slot].reshape(nk, ATT_HEAD_DIM).astype(BF16)
    vb = vbuf[slot].reshape(nk, ATT_HEAD_DIM).astype(BF16)
    sc = lax.dot_general(q8, kb, nt, preferred_element_type=F32)[0:1, :] * scale
    bias = jnp.concatenate(
        [jnp.where(sel_ref[b, h, j] == last_blk, bl_ref[0], bl_ref[0, :, 0:1]) for j in range(MOBA_TOPK)], axis=1)
    sc = sc + bias
    s_own = jnp.sum(q8[0:1].astype(F32) * kn_ref[0].astype(BF16).astype(F32), axis=-1, keepdims=True) * scale \
        + b0_ref[0]
    m = jnp.maximum(jnp.max(sc, axis=-1, keepdims=True), s_own)
    p = jnp.exp(sc - m)
    p_own = jnp.exp(s_own - m)
    l = jnp.sum(p, axis=-1, keepdims=True) + p_own
    p8 = jnp.broadcast_to(p, (SUBLANE, nk)).astype(BF16)
    acc = jnp.dot(p8, vb, preferred_element_type=F32)[0:1, :] + p_own * vn_ref[0].astype(BF16).astype(F32)
    o_ref[0] = (acc / l).astype(o_ref.dtype)


def _att_sample(page_table, sel, proj3, cache_k, cache_v, layer, bias0, bias_last):
    nd, n_pages = page_table.shape
    page = cache_k.shape[2]
    ppb = MOBA_BLOCK // page
    nblk = n_pages // ppb
    hd = ATT_HEAD_DIM
    qc, kc, vc = R_Q // hd, R_K // hd, R_V // hd
    kern = functools.partial(_att_s_kernel, layer=layer, last_blk=nblk - 1, page=page)
    tok = lambda c0: pl.BlockSpec((1, 1, hd), lambda s, pt, sl: (s // ATT_HEADS, 0, c0 + s % ATT_HEADS))
    return pl.pallas_call(
        kern,
        grid_spec=pltpu.PrefetchScalarGridSpec(
            num_scalar_prefetch=2,
            grid=(nd * ATT_HEADS,),
            in_specs=[
                tok(qc), tok(kc), tok(vc),
                pl.BlockSpec((1, 1, 1), lambda s, pt, sl: (s % ATT_HEADS, 0, 0)),
                pl.BlockSpec((1, 1, MOBA_BLOCK), lambda s, pt, sl: (s % ATT_HEADS, 0, 0)),
                pl.BlockSpec(memory_space=pl.ANY),
                pl.BlockSpec(memory_space=pl.ANY),
            ],
            out_specs=pl.BlockSpec((1, 1, hd), lambda s, pt, sl: (s // ATT_HEADS, 0, s % ATT_HEADS)),
            scratch_shapes=[
                pltpu.VMEM((2, MOBA_TOPK, MOBA_BLOCK, hd), F32),
                pltpu.VMEM((2, MOBA_TOPK, MOBA_BLOCK, hd), F32),
                pltpu.SemaphoreType.DMA((2, 2, MOBA_TOPK, ppb)),
            ],
        ),
        out_shape=jax.ShapeDtypeStruct((nd, 1, ATT_WIDTH), BF16),
        compiler_params=_params("arbitrary"),
    )(page_table, sel, proj3, proj3, proj3, bias0, bias_last, cache_k, cache_v)


def _merge_kernel(y0_ref, y1_ref, y2_ref, w0_ref, w1_ref, w2_ref, g0_ref, g1_ref, g2_ref, o_ref):
    acc = None
    for y_ref, w_ref, g_ref in ((y0_ref, w0_ref, g0_ref), (y1_ref, w1_ref, g1_ref), (y2_ref, w2_ref, g2_ref)):
        t = jnp.dot(y_ref[...], w_ref[...], preferred_element_type=F32)
        t = t * (1.0 / (1.0 + jnp.exp(-g_ref[...])))
        acc = t if acc is None else acc + t
    o_ref[...] = acc.astype(o_ref.dtype)


def _merge(ys, ws, proj, tm, tn):
    m = ys[0].shape[0]
    kd = ys[0].shape[1]
    gc = R_GATE // tn
    gstep = D_MODEL // tn
    yspec = pl.BlockSpec((tm, kd), lambda i, j: (i, 0))
    wspec = pl.BlockSpec((kd, tn), lambda i, j: (0, j))
    gspec = lambda br: pl.BlockSpec((tm, tn), lambda i, j: (i, gc + br * gstep + j))
    return pl.pallas_call(
        _merge_kernel,
        grid=(m // tm, D_MODEL // tn),
        in_specs=[yspec, yspec, yspec, wspec, wspec, wspec, gspec(0), gspec(1), gspec(2)],
        out_specs=pl.BlockSpec((tm, tn), lambda i, j: (i, j)),
        out_shape=jax.ShapeDtypeStruct((m, D_MODEL), BF16),
        compiler_params=_params("parallel", "arbitrary"),
    )(*ys, *ws, proj, proj, proj)


def _ln(x, g, b):
    mu = jnp.mean(x, axis=-1, keepdims=True)
    xc = x - mu
    var = jnp.mean(xc * xc, axis=-1, keepdims=True)
    return xc * lax.rsqrt(var + LN_EPS) * g + b


def _mm_ln_kernel(a_ref, w_ref, r_ref, g_ref, b_ref, o_ref, ob_ref):
    k = pl.program_id(1)
    part = jnp.dot(a_ref[...], w_ref[...], preferred_element_type=F32)

    @pl.when(k == 0)
    def _():
        o_ref[...] = DEEPNORM_ALPHA * r_ref[...] + part

    @pl.when(k > 0)
    def _():
        o_ref[...] += part

    @pl.when(k == pl.num_programs(1) - 1)
    def _():
        y = _ln(o_ref[...], g_ref[...], b_ref[...])
        o_ref[...] = y
        ob_ref[...] = y.astype(BF16)


def _matmul_ln(a, w, res, g, b, tm, tk):
    m, kd = a.shape
    n = w.shape[1]
    return pl.pallas_call(
        _mm_ln_kernel,
        grid=(m // tm, kd // tk),
        in_specs=[
            pl.BlockSpec((tm, tk), lambda i, k: (i, k)),
            pl.BlockSpec((tk, n), lambda i, k: (k, 0)),
            pl.BlockSpec((tm, n), lambda i, k: (i, 0)),
            pl.BlockSpec((1, n), lambda i, k: (0, 0)),
            pl.BlockSpec((1, n), lambda i, k: (0, 0)),
        ],
        out_specs=[pl.BlockSpec((tm, n), lambda i, k: (i, 0)),
                   pl.BlockSpec((tm, n), lambda i, k: (i, 0))],
        out_shape=[jax.ShapeDtypeStruct((m, n), F32), jax.ShapeDtypeStruct((m, n), BF16)],
        compiler_params=_params("parallel", "arbitrary"),
    )(a, w, res, g, b)


def _add_ln_t_kernel(r_ref, ft_ref, g_ref, b_ref, o_ref, ob_ref):
    y = _ln(DEEPNORM_ALPHA * r_ref[...] + ft_ref[...].T, g_ref[...], b_ref[...])
    o_ref[...] = y
    ob_ref[...] = y.astype(BF16)


def _add_ln_t(res, f_t, g, b, tm):
    m, n = res.shape
    row = pl.BlockSpec((tm, n), lambda i: (i, 0))
    vec = pl.BlockSpec((1, n), lambda i: (0, 0))
    return pl.pallas_call(
        _add_ln_t_kernel,
        grid=(m // tm,),
        in_specs=[row, pl.BlockSpec((n, tm), lambda i: (0, i)), vec, vec],
        out_specs=[row, row],
        out_shape=[jax.ShapeDtypeStruct((m, n), F32), jax.ShapeDtypeStruct((m, n), BF16)],
        compiler_params=_params("parallel"),
    )(res, f_t, g, b)


def _top_desc(vals, k):
    n = vals.shape[0]
    row = lax.broadcasted_iota(jnp.int32, vals.shape, 0)
    tops, idxs = [], []
    for _ in range(k):
        m = jnp.max(vals, axis=0, keepdims=True)
        idx = jnp.min(jnp.where(vals == m, row, n), axis=0, keepdims=True)
        vals = jnp.where(row == idx, -jnp.inf, vals)
        tops.append(m)
        idxs.append(idx)
    return tops, idxs


def _route_kernel(q_ref, sk_ref, s2_ref, e2_ref, thr_ref, e1_ref):
    nt = (((1,), (1,)), ((), ()))
    for h in range(PEER_HEADS):
        hs = slice(h * PEER_KEYS, (h + 1) * PEER_KEYS)
        st = []
        for c in range(2):
            col = (h * 2 + c) * PEER_HALF
            st.append(lax.dot_general(sk_ref[h, c], q_ref[:, col:col + PEER_HALF], nt,
                                      preferred_element_type=F32))
        u, ui = _top_desc(st[0], PEER_TOPK)
        v, _ = _top_desc(st[1], PEER_TOPK)
        vcat = jnp.concatenate(v, axis=0)
        cand = jnp.concatenate([u_i + vcat for u_i in u], axis=0)
        t, _ = _top_desc(cand, PEER_TOPK)
        tau = t[PEER_TOPK - 1]
        z = jnp.ones_like(t[0])
        for tk in t[1:]:
            z = z + jnp.exp(tk - t[0])
        row = lax.broadcasted_iota(jnp.int32, st[0].shape, 0)
        thr = jnp.full(st[0].shape, jnp.inf, F32)
        for i in range(PEER_TOPK):
            ti = jnp.min(jnp.where(u[i] + vcat >= tau, vcat, jnp.inf), axis=0, keepdims=True)
            thr = jnp.where(row == ui[i], ti, thr)
        s2_ref[hs, :] = st[1]
        e2_ref[hs, :] = jnp.exp(st[1] - v[0])
        thr_ref[hs, :] = thr
        e1_ref[hs, :] = jnp.exp(st[0] - u[0]) / z


def _route(qp, subkeys, tb):
    t = qp.shape[0]
    w = PEER_HEADS * PEER_KEYS
    col = pl.BlockSpec((w, tb), lambda i: (0, i))
    sds = jax.ShapeDtypeStruct((w, t), F32)
    return pl.pallas_call(
        _route_kernel,
        grid=(t // tb,),
        in_specs=[pl.BlockSpec((tb, PEER_HEADS * PEER_QDIM), lambda i: (i, 0)),
                  pl.BlockSpec((PEER_HEADS, 2, PEER_KEYS, PEER_HALF), lambda i: (0, 0, 0, 0))],
        out_specs=[col, col, col, col],
        out_shape=[sds, sds, sds, sds],
        compiler_params=_params("parallel"),
    )(qp, subkeys)


def _gelu_tanh(x):
    return 0.5 * x * (1.0 + jnp.tanh(math.sqrt(2.0 / math.pi) * (x + 0.044715 * (x * x * x))))


def _peer_kernel(x_ref, u_ref, vt_ref, s2_ref, e2_ref, thr_ref, e1_ref, o_ref, *, te):
    j = pl.program_id(1)
    na = te // PEER_KEYS
    nt = (((1,), (1,)), ((), ()))

    @pl.when(j == 0)
    def _():
        o_ref[...] = jnp.zeros_like(o_ref)

    hid = lax.dot_general(u_ref[...], x_ref[...], nt, preferred_element_type=F32)
    pieces = []
    for ai in range(na):
        a = j * na + ai
        g = None
        for h in range(PEER_HEADS):
            hs = slice(h * PEER_KEYS, (h + 1) * PEER_KEYS)
            thr_row = thr_ref[pl.ds(h * PEER_KEYS + a, 1), :]
            e1_row = e1_ref[pl.ds(h * PEER_KEYS + a, 1), :]
            t = jnp.where(s2_ref[hs, :] >= thr_row, e2_ref[hs, :], 0.0) * e1_row
            g = t if g is None else g + t
        pieces.append(g * _gelu_tanh(hid[ai * PEER_KEYS:(ai + 1) * PEER_KEYS, :]))
    wgt = jnp.concatenate(pieces, axis=0).astype(BF16)
    o_ref[...] += jnp.dot(vt_ref[...], wgt, preferred_element_type=F32)


def _peer_dense(xb, u_tab, vt_tab, s2t, e2t, thr, e1, tb, te):
    t, d = xb.shape
    ne = u_tab.shape[0]
    w = PEER_HEADS * PEER_KEYS
    kern = functools.partial(_peer_kernel, te=te)
    once = pl.Buffered(1)
    col = pl.BlockSpec((w, tb), lambda i, j: (0, i), pipeline_mode=once)
    return pl.pallas_call(
        kern,
        grid=(t // tb, ne // te),
        in_specs=[
            pl.BlockSpec((tb, d), lambda i, j: (i, 0), pipeline_mode=once),
            pl.BlockSpec((te, d), lambda i, j: (j, 0)),
            pl.BlockSpec((d, te), lambda i, j: (0, j)),
            col, col, col, col,
        ],
        out_specs=pl.BlockSpec((d, tb), lambda i, j: (0, i)),
        out_shape=jax.ShapeDtypeStruct((d, t), F32),
        compiler_params=_params("parallel", "arbitrary"),
    )(xb, u_tab, vt_tab, s2t, e2t, thr, e1)


def _prep_layer(i, w_in, conv_w, conv_b, dt_bias, a_log, d_skip, ssd_norm_w, pool_w, pool_scale,
                w_br_ssd, w_br_pool, w_br_att, w_out, ln1_g, ln1_b, peer_wq, peer_subkeys,
                peer_u, peer_v, ln2_g, ln2_b):
    w = w_in[i]
    xs_end = OFF_XBC + SSD_INNER
    w_r = jnp.concatenate([
        w[:, OFF_Z:OFF_XBC], w[:, OFF_XBC:xs_end], w[:, OFF_POOL:OFF_Q], w[:, OFF_Q:OFF_K],
        w[:, OFF_K:OFF_V], w[:, OFF_V:OFF_GATE], w[:, OFF_GATE:PROJ_DIM], w[:, xs_end:OFF_DT],
        w[:, OFF_DT:OFF_POOL],
        jnp.zeros((D_MODEL, R_DIM - R_DT - SSD_HEADS), w.dtype)], axis=1).astype(BF16)
    padh = lambda v: jnp.pad(v.astype(F32), (0, LANE - SSD_HEADS))[None, :]
    row = lambda v: v.astype(F32)[None, :]
    return {
        "w_in": w_r,
        "conv_w_t": conv_w[i].astype(F32).T,
        "conv_b": row(conv_b[i]),
        "dt_bias": padh(dt_bias[i]),
        "a_log": padh(a_log[i]),
        "d_skip": row(jnp.repeat(d_skip[i], SSD_HEAD_DIM)),
        "ssd_norm_w": row(ssd_norm_w[i]),
        "pool_w": pool_w[i].astype(BF16),
        "pool_scale": row(pool_scale[i]),
        "w_br": (w_br_ssd[i].astype(BF16), w_br_pool[i].astype(BF16), w_br_att[i].astype(BF16)),
        "w_out": w_out[i].astype(BF16),
        "ln1_g": row(ln1_g[i]), "ln1_b": row(ln1_b[i]),
        "peer_wq": peer_wq[i].astype(BF16),
        "peer_subkeys": peer_subkeys[i].astype(BF16),
        "peer_u": peer_u[i].astype(BF16),
        "peer_vt": peer_v[i].astype(BF16).T,
        "ln2_g": row(ln2_g[i]), "ln2_b": row(ln2_b[i]),
    }


def _tail(x, lp, tm, tb, te):
    ys, proj, x_f32 = x
    merged = _merge(ys, lp["w_br"], proj, tm, 512)
    x1, x1b = _matmul_ln(merged, lp["w_out"], x_f32, lp["ln1_g"], lp["ln1_b"], min(tm, 256), 1024)
    qp = _matmul(x1b, lp["peer_wq"], BF16, tm, 512)
    s2t, e2t, thr, e1 = _route(qp, lp["peer_subkeys"], min(tm, 256))
    ffn_t = _peer_dense(x1b, lp["peer_u"], lp["peer_vt"], s2t, e2t, thr, e1, tb, te)
    return _add_ln_t(x1, ffn_t, lp["ln2_g"], lp["ln2_b"], min(tm, 256))


def kernel(x_prompt, x_sample, cache_k, cache_v, state_ssm, state_conv, state_pool, page_table, w_in, conv_w, conv_b, dt_bias, a_log, d_skip, ssd_norm_w, pool_w, pool_scale, rel_bias, w_br_ssd, w_br_pool, w_br_att, w_out, ln1_g, ln1_b, peer_wq, peer_subkeys, peer_u, peer_v, ln2_g, ln2_b):
    bp, seq, _ = x_prompt.shape
    nd = x_sample.shape[0]
    n_pool, _, page, _, _ = cache_k.shape
    n_pages = page_table.shape[1]
    past_len = n_pages * page
    npair = SSD_HEADS // 2
    tp = bp * seq
    ts = LANE
    assert seq % MOBA_BLOCK == 0 and seq % CHUNK == 0 and nd <= ts and tp % 1024 == 0
    assert past_len % MOBA_BLOCK == 0 and MOBA_BLOCK % page == 0

    rb = rel_bias.astype(F32)
    ii = jnp.arange(MOBA_BLOCK)
    d0 = ii[:, None] - ii[None, :]
    bias_tiles = jnp.stack([rb[_rel_bucket(d0)], rb[_rel_bucket(d0 + MOBA_BLOCK)]], axis=0)
    bias_tiles = jnp.transpose(bias_tiles, (3, 0, 1, 2))
    bias0 = rb[_rel_bucket(jnp.zeros((1,), jnp.int32))].T[:, :, None]
    bias_last = rb[_rel_bucket(MOBA_BLOCK - ii)].T[:, None, :]

    cache_k_rows = cache_k.reshape(n_pool, DEPTH, page * ATT_HEADS, ATT_HEAD_DIM)

    xp = x_prompt.reshape(tp, D_MODEL)
    xpb = xp.astype(BF16)
    xs = jnp.pad(x_sample.reshape(nd, D_MODEL), ((0, ts - nd), (0, 0)))
    xsb = xs.astype(BF16)

    zero_conv = jnp.zeros((bp, CONV_TAIL, SSD_CONV_DIM), F32)
    zero_ssm = jnp.zeros((bp, npair, LANE, SSD_STATE), F32)
    zero_pool = jnp.zeros((bp, POOL_TAIL, POOL_WIDTH), F32)

    outs = {k: [] for k in ("kp", "vp", "ks", "vs", "hp", "hs", "cp", "cs", "pp", "ps")}
    for i in range(DEPTH):
        lp = _prep_layer(i, w_in, conv_w, conv_b, dt_bias, a_log, d_skip, ssd_norm_w, pool_w, pool_scale,
                         w_br_ssd, w_br_pool, w_br_att, w_out, ln1_g, ln1_b, peer_wq, peer_subkeys,
                         peer_u, peer_v, ln2_g, ln2_b)

        proj = _matmul(xpb, lp["w_in"], F32, 1024, PROJ_TN)
        nc = seq // CHUNK
        y_ssd, h_new, c_new = _ssd(proj, proj, proj, proj,
                                   (R_Z // SSD_INNER, R_XS // SSD_INNER, R_BC // 1024, R_DT // LANE),
                                   bp, nc, CHUNK, zero_conv, zero_ssm, lp)
        y_pool, p_new = _pool(proj, R_POOL // POOL_WIDTH, bp, nc, CHUNK, 0, zero_pool, lp)
        y_att = _moba_prompt(proj, bp, seq, bias_tiles)
        xp, xpb = _tail(((y_ssd, y_pool, y_att), proj, xp), lp, 512, 512, 512)
        outs["kp"].append(proj[:, R_K:R_K + ATT_WIDTH].reshape(bp, seq, ATT_HEADS, ATT_HEAD_DIM))
        outs["vp"].append(proj[:, R_V:R_V + ATT_WIDTH].reshape(bp, seq, ATT_HEADS, ATT_HEAD_DIM))
        outs["hp"].append(h_new.reshape(bp, SSD_HEADS, SSD_HEAD_DIM, SSD_STATE))
        outs["cp"].append(c_new)
        outs["pp"].append(p_new)

        proj_s = _matmul(xsb, lp["w_in"], F32, ts, PROJ_TN)
        ps = proj_s[:nd]
        chunked = lambda a: jnp.pad(a[:, None, :], ((0, 0), (0, CHUNK - 1), (0, 0))).reshape(nd * CHUNK, a.shape[1])
        cbuf = jnp.pad(state_conv[i].astype(F32), ((0, 0), (CONV_TAIL - (SSD_CONV - 1), 0), (0, 0)))
        pbuf = jnp.pad(state_pool[i].astype(F32), ((0, 0), (POOL_TAIL - POOL_BUF, 0), (0, 0)))
        y_ssd_s, h_new_s, c_new_s = _ssd(
            chunked(ps[:, R_Z:R_Z + SSD_INNER]), chunked(ps[:, R_XS:R_XS + SSD_INNER]),
            chunked(ps[:, R_BC:R_DT]), chunked(ps[:, R_DT:R_END]), (0, 0, 0, 0),
            nd, 1, 1, cbuf, state_ssm[i].astype(F32).reshape(nd, npair, LANE, SSD_STATE), lp)
        y_pool_s, p_new_s = _pool(chunked(ps[:, R_POOL:R_POOL + POOL_WIDTH]), 0, nd, 1, 1, past_len, pbuf, lp)
        proj3 = ps.reshape(nd, 1, R_DIM)
        q_heads = ps[:, R_Q:R_Q + ATT_WIDTH].reshape(nd, ATT_HEADS, ATT_HEAD_DIM)
        _, sel = _gate_sample(page_table, q_heads, cache_k_rows, i)
        y_att_s = _att_sample(page_table, sel[:, :, :MOBA_TOPK], proj3, cache_k, cache_v, i, bias0, bias_last)
        first = lambda y: jnp.pad(y.reshape(nd, CHUNK, -1)[:, 0], ((0, ts - nd), (0, 0)))
        ys_s = (first(y_ssd_s), first(y_pool_s), jnp.pad(y_att_s.reshape(nd, ATT_WIDTH), ((0, ts - nd), (0, 0))))
        xs, xsb = _tail((ys_s, proj_s, xs), lp, ts, ts, 512)
        outs["ks"].append(ps[:, R_K:R_K + ATT_WIDTH].reshape(nd, 1, ATT_HEADS, ATT_HEAD_DIM))
        outs["vs"].append(ps[:, R_V:R_V + ATT_WIDTH].reshape(nd, 1, ATT_HEADS, ATT_HEAD_DIM))
        outs["hs"].append(h_new_s.reshape(nd, SSD_HEADS, SSD_HEAD_DIM, SSD_STATE))
        outs["cs"].append(c_new_s)
        outs["ps"].append(p_new_s)

    return (xp.reshape(bp, seq, D_MODEL), xs[:nd].reshape(nd, 1, D_MODEL),
            jnp.stack(outs["kp"], axis=1), jnp.stack(outs["vp"], axis=1),
            jnp.stack(outs["ks"], axis=1), jnp.stack(outs["vs"], axis=1),
            jnp.stack(outs["hp"], axis=0), jnp.stack(outs["hs"], axis=0),
            jnp.stack(outs["cp"], axis=0), jnp.stack(outs["cs"], axis=0),
            jnp.stack(outs["pp"], axis=0), jnp.stack(outs["ps"], axis=0))
```
